```python
import math
import jax, jax.numpy as jnp
from jax import lax
import numpy as np

D_MODEL = 1024
BATCH = 16
SEQ = 4096
DEPTH = 1

HEAD_DIM = 64
HALF_DIM = HEAD_DIM // 2
A_HEADS = 8
A_WIDTH = A_HEADS * HEAD_DIM
DILATED_PATTERNS = ((128, 1), (512, 4), (2048, 16))
B_HEADS = 4
B_QK_WIDTH = B_HEADS * 2 * HEAD_DIM
B_V_WIDTH = B_HEADS * 2 * HEAD_DIM
IN_WIDTHS = (A_WIDTH, A_WIDTH, A_WIDTH, B_QK_WIDTH, B_QK_WIDTH, B_V_WIDTH, D_MODEL, D_MODEL)
IN_WIDTH = 3 * A_WIDTH + 2 * B_QK_WIDTH + B_V_WIDTH + 2 * D_MODEL
D_FF = 4 * D_MODEL
ROPE_THETA = 10000.0
NORM_EPS = 1e-6
SUBLN_EPS = 1e-5
Q_BLOCK = 128

kernel_name = "hybrid_dilated_diffattn_gated_block"


def lambda_init_fn(layer):
    return 0.8 - 0.6 * math.exp(-0.3 * layer)


def rms_norm(x, g, eps):
    xf = x.astype(jnp.float32)
    y = xf * lax.rsqrt(jnp.mean(xf * xf, axis=-1, keepdims=True) + eps)
    return (y * g.astype(jnp.float32)).astype(x.dtype)


def rope_tables(seq, dtype):
    pos = jnp.arange(seq, dtype=jnp.float32)
    inv_freq = ROPE_THETA ** (-jnp.arange(0, HEAD_DIM, 2, dtype=jnp.float32) / HEAD_DIM)
    ang = pos[:, None] * inv_freq[None, :]
    ang = jnp.concatenate([ang, ang], axis=-1)
    return jnp.cos(ang).astype(dtype), jnp.sin(ang).astype(dtype)


def apply_rope(t, cos, sin):
    shp = t.shape
    t4 = t.reshape(shp[0], shp[1], -1, HEAD_DIM)
    rot = jnp.concatenate([-t4[..., HALF_DIM:], t4[..., :HALF_DIM]], axis=-1)
    out = t4 * cos[None, :, None, :] + rot * sin[None, :, None, :]
    return out.reshape(shp)


def banded_window_attention(q, k, v, half):
    n, L, h, dh = q.shape
    blk = half
    nb = -(-L // blk)
    lp = nb * blk
    qb = jnp.pad(q, ((0, 0), (0, lp - L), (0, 0), (0, 0))).reshape(n, nb, blk, h, dh)
    pad_kv = ((0, 0), (blk, lp - L + blk), (0, 0), (0, 0))
    kp = jnp.pad(k, pad_kv).reshape(n, nb + 2, blk, h, dh)
    vp = jnp.pad(v, pad_kv).reshape(n, nb + 2, blk, h, dh)
    kw = jnp.concatenate([kp[:, :-2], kp[:, 1:-1], kp[:, 2:]], axis=2)
    vw = jnp.concatenate([vp[:, :-2], vp[:, 1:-1], vp[:, 2:]], axis=2)
    qpos = jnp.arange(nb)[:, None] * blk + jnp.arange(blk)[None, :]
    kpos = (jnp.arange(nb)[:, None] - 1) * blk + jnp.arange(3 * blk)[None, :]
    rel = kpos[:, None, :] - qpos[:, :, None]
    valid = (jnp.abs(rel) <= half) & (kpos[:, None, :] >= 0) & (kpos[:, None, :] < L)
    scale = 1.0 / math.sqrt(dh)
    s = jnp.einsum('ncqhd,nckhd->nhcqk', qb, kw).astype(jnp.float32) * scale
    s = jnp.where(valid[None, None], s, -jnp.inf)
    lse = jax.nn.logsumexp(s, axis=-1)
    p = jnp.exp(s - lse[..., None])
    o = jnp.einsum('nhcqk,nckhd->ncqhd', p.astype(v.dtype), vw).reshape(n, lp, h, dh)[:, :L]
    lse = lse.transpose(0, 2, 3, 1).reshape(n, lp, h)[:, :L]
    return o, lse


def dilated_window_attention(q, k, v):
    b, s, h, dh = q.shape
    outs, lses = [], []
    for window, dil in DILATED_PATTERNS:
        half = window // (2 * dil)
        ls = s // dil

        def to_phase(t):
            return t.reshape(b, ls, dil, h, dh).transpose(0, 2, 1, 3, 4).reshape(b * dil, ls, h, dh)

        o, lse = banded_window_attention(to_phase(q), to_phase(k), to_phase(v), half)
        outs.append(o.reshape(b, dil, ls, h, dh).transpose(0, 2, 1, 3, 4).reshape(b, s, h, dh))
        lses.append(lse.reshape(b, dil, ls, h).transpose(0, 2, 1, 3).reshape(b, s, h))
    w = jax.nn.softmax(jnp.stack(lses, axis=0), axis=0)
    return jnp.einsum('pbsh,pbshd->bshd', w.astype(q.dtype), jnp.stack(outs, axis=0))


def differential_attention(q, k, v, lam, lam_init, subln_g):
    b, s, h, _, dh = q.shape
    nq = s // Q_BLOCK
    scale = 1.0 / math.sqrt(dh)
    qblocks = q.reshape(b, nq, Q_BLOCK, h, 2, dh).transpose(1, 0, 2, 3, 4, 5)

    def block(qb):
        sc = jnp.einsum('bqhcd,bkhcd->bhcqk', qb, k).astype(jnp.float32) * scale
        p = jax.nn.softmax(sc, axis=-1)
        pd = p[:, :, 0] - lam * p[:, :, 1]
        return jnp.einsum('bhqk,bkhe->bqhe', pd.astype(v.dtype), v)

    o = lax.map(block, qblocks)
    o = o.transpose(1, 0, 2, 3, 4).reshape(b, s, h, 2 * dh)
    return rms_norm(o, subln_g, SUBLN_EPS) * (1.0 - lam_init)


def setup_inputs(seed: int = 0) -> dict:
    key = jax.random.key(seed)
    ks = jax.random.split(key, 16)
    f32 = jnp.float32

    def nrm(k, shape, scale):
        return jax.random.normal(k, shape, f32) * scale

    return {
        "x": nrm(ks[0], (BATCH, SEQ, D_MODEL), 1.0),
        "w_in": nrm(ks[1], (DEPTH, D_MODEL, IN_WIDTH), D_MODEL ** -0.5),
        "w_branch_a": nrm(ks[2], (DEPTH, A_WIDTH, D_MODEL), A_WIDTH ** -0.5),
        "w_branch_b": nrm(ks[3], (DEPTH, B_V_WIDTH, D_MODEL), B_V_WIDTH ** -0.5),
        "w_out": nrm(ks[4], (DEPTH, D_MODEL, D_MODEL), D_MODEL ** -0.5),
        "lambda_q1": nrm(ks[5], (DEPTH, HEAD_DIM), 0.1),
        "lambda_k1": nrm(ks[6], (DEPTH, HEAD_DIM), 0.1),
        "lambda_q2": nrm(ks[7], (DEPTH, HEAD_DIM), 0.1),
        "lambda_k2": nrm(ks[8], (DEPTH, HEAD_DIM), 0.1),
        "diff_subln_g": 1.0 + nrm(ks[9], (DEPTH, 2 * HEAD_DIM), 0.02),
        "norm_mix_g": 1.0 + nrm(ks[10], (DEPTH, D_MODEL), 0.02),
        "norm_mlp_g": 1.0 + nrm(ks[11], (DEPTH, D_MODEL), 0.02),
        "w_ff1": nrm(ks[12], (DEPTH, D_MODEL, D_FF), D_MODEL ** -0.5),
        "w_ff2": nrm(ks[13], (DEPTH, D_FF, D_MODEL), D_FF ** -0.5),
        "norm_final_g": 1.0 + nrm(ks[14], (D_MODEL,), 0.02),
    }


def reference(x, w_in, w_branch_a, w_branch_b, w_out, lambda_q1, lambda_k1, lambda_q2, lambda_k2,
              diff_subln_g, norm_mix_g, norm_mlp_g, w_ff1, w_ff2, norm_final_g):
    b, s, _ = x.shape
    cos, sin = rope_tables(s, x.dtype)
    split_idx = np.cumsum(IN_WIDTHS)[:-1].tolist()
    for l in range(DEPTH):
        h_in = rms_norm(x, norm_mix_g[l], NORM_EPS)
        proj = h_in @ w_in[l]
        qa, ka, va, qb, kb, vb, ga, gb = jnp.split(proj, split_idx, axis=-1)
        qa = apply_rope(qa.reshape(b, s, A_HEADS, HEAD_DIM), cos, sin)
        ka = apply_rope(ka.reshape(b, s, A_HEADS, HEAD_DIM), cos, sin)
        va = va.reshape(b, s, A_HEADS, HEAD_DIM)
        qb = apply_rope(qb.reshape(b, s, B_HEADS, 2, HEAD_DIM), cos, sin)
        kb = apply_rope(kb.reshape(b, s, B_HEADS, 2, HEAD_DIM), cos, sin)
        vb = vb.reshape(b, s, B_HEADS, 2 * HEAD_DIM)

        ya = dilated_window_attention(qa, ka, va).reshape(b, s, A_WIDTH) @ w_branch_a[l]

        lam_init = lambda_init_fn(l)
        lam = (jnp.exp(jnp.sum(lambda_q1[l].astype(jnp.float32) * lambda_k1[l].astype(jnp.float32)))
               - jnp.exp(jnp.sum(lambda_q2[l].astype(jnp.float32) * lambda_k2[l].astype(jnp.float32)))
               + lam_init)
        yb = differential_attention(qb, kb, vb, lam, lam_init, diff_subln_g[l]).reshape(b, s, B_V_WIDTH) @ w_branch_b[l]

        merged = jax.nn.sigmoid(ga) * ya + jax.nn.sigmoid(gb) * yb
        x = x + merged @ w_out[l]

        h2 = rms_norm(x, norm_mlp_g[l], NORM_EPS)
        x = x + jnp.square(jax.nn.relu(h2 @ w_ff1[l])) @ w_ff2[l]
    return rms_norm(x, norm_final_g, NORM_EPS)
```

```python
import functools
import math

import jax
import jax.numpy as jnp
from jax import lax
from jax.experimental import pallas as pl
from jax.experimental.pallas import tpu as pltpu

HEAD_DIM = 64
HALF_DIM = HEAD_DIM // 2
LANES = 128
A_HEADS = 8
A_WIDTH = A_HEADS * HEAD_DIM
B_HEADS = 4
B_WIDTH = B_HEADS * 2 * HEAD_DIM
QKV_WIDTH = 3 * A_WIDTH + 3 * B_WIDTH
DILATED_PATTERNS = ((128, 1), (512, 4), (2048, 16))
ROPE_THETA = 10000.0
NORM_EPS = 1e-6
SUBLN_EPS = 1e-5
LAMBDA_INIT = 0.8 - 0.6 * math.exp(-0.3 * 0)
QK_SCALE = 1.0 / math.sqrt(HEAD_DIM)
MASK_VALUE = -1e30
VMEM_LIMIT_BYTES = 56 * 1024 * 1024

BF16 = jnp.bfloat16
F32 = jnp.float32


def _rms_norm(x, g, eps):
    return (x * lax.rsqrt(jnp.mean(x * x, axis=-1, keepdims=True) + eps)) * g


def _split_heads_rows(q):
    lane = lax.broadcasted_iota(jnp.int32, q.shape, 1)
    zero = jnp.zeros_like(q)
    return jnp.concatenate(
        [jnp.where(lane < HEAD_DIM, q, zero), jnp.where(lane >= HEAD_DIM, q, zero)], axis=0)


def _qkv_kernel(x_ref, g_ref, w_ref, cos_ref, sin_ref, o_ref):
    x = x_ref[0]
    hb = _rms_norm(x, g_ref[...], NORM_EPS).astype(BF16)
    cos = cos_ref[...]
    sin = sin_ref[...]
    lane = lax.broadcasted_iota(jnp.int32, cos.shape, 1)
    first_half = (lane % HEAD_DIM) < HALF_DIM
    n_chunks = QKV_WIDTH // A_WIDTH
    for c in range(n_chunks):
        t = jnp.dot(hb, w_ref[:, c * A_WIDTH:(c + 1) * A_WIDTH], preferred_element_type=F32)
        if c in (2, 5):
            o_ref[0, :, c * A_WIDTH:(c + 1) * A_WIDTH] = t.astype(BF16)
            continue
        for j in range(A_WIDTH // LANES):
            tj = t[:, j * LANES:(j + 1) * LANES]
            rot = jnp.where(first_half, pltpu.roll(tj, LANES - HALF_DIM, 1), pltpu.roll(tj, HALF_DIM, 1))
            tj = tj * cos + rot * sin
            if c in (0, 3):
                tj = tj * QK_SCALE
            lo = c * A_WIDTH + j * LANES
            o_ref[0, :, lo:lo + LANES] = tj.astype(BF16)


def _qkv_proj(x, g, w_qkv, cos, sin, ts):
    b, s, d = x.shape
    return pl.pallas_call(
        _qkv_kernel,
        grid=(b, s // ts),
        in_specs=[
            pl.BlockSpec((1, ts, d), lambda i, j: (i, j, 0)),
            pl.BlockSpec((1, d), lambda i, j: (0, 0)),
            pl.BlockSpec((d, QKV_WIDTH), lambda i, j: (0, 0)),
            pl.BlockSpec((ts, LANES), lambda i, j: (j, 0)),
            pl.BlockSpec((ts, LANES), lambda i, j: (j, 0)),
        ],
        out_specs=pl.BlockSpec((1, ts, QKV_WIDTH), lambda i, j: (i, j, 0)),
        out_shape=jax.ShapeDtypeStruct((b, s, QKV_WIDTH), BF16),
        compiler_params=pltpu.CompilerParams(
            dimension_semantics=("arbitrary", "arbitrary"), vmem_limit_bytes=VMEM_LIMIT_BYTES),
        name="qkv_proj",
    )(x, g, w_qkv, cos, sin)


A_TQ = 128
A_WIN = 256


def _band_kernel(q_ref, k_ref, v_ref, o_ref, lse_ref, *, half, ls, rows_per_step):
    step = pl.program_id(2)
    n_sub = rows_per_step // A_TQ
    row = lax.broadcasted_iota(jnp.int32, (2 * A_TQ, A_WIN), 0)
    col = lax.broadcasted_iota(jnp.int32, (2 * A_TQ, A_WIN), 1)
    rel0 = col - (row % A_TQ)
    out_lane = lax.broadcasted_iota(jnp.int32, (A_TQ, LANES), 1)

    def sub_tile(i, carry):
        r0 = pl.multiple_of(i * A_TQ, A_TQ)
        t0 = step * rows_per_step + r0
        start = pl.multiple_of(jnp.clip(t0 - half, 0, ls - A_WIN), half)
        bias = jnp.where(jnp.abs(rel0 + (start - t0)) <= half, 0.0, MASK_VALUE).astype(F32)
        lse_tile = jnp.zeros((A_TQ, LANES), F32)
        for j in range(A_WIDTH // LANES):
            cs = slice(j * LANES, (j + 1) * LANES)
            qq = _split_heads_rows(q_ref[0, pl.ds(r0, A_TQ), cs])
            kw = k_ref[0, pl.ds(start, A_WIN), cs]
            vw = v_ref[0, pl.ds(start, A_WIN), cs]
            s = lax.dot_general(qq, kw, (((1,), (1,)), ((), ())), preferred_element_type=F32) + bias
            m = jnp.max(s, axis=-1, keepdims=True)
            p = jnp.exp(s - m)
            l = jnp.sum(p, axis=-1, keepdims=True)
            pv = jnp.dot(p.astype(BF16), vw, preferred_element_type=F32) * (1.0 / l)
            o_ref[0, pl.ds(r0, A_TQ), cs] = jnp.where(
                out_lane < HEAD_DIM, pv[:A_TQ], pv[A_TQ:]).astype(BF16)
            lse = m + jnp.log(l)
            lse_tile = jnp.where(out_lane == 2 * j, lse[:A_TQ], lse_tile)
            lse_tile = jnp.where(out_lane == 2 * j + 1, lse[A_TQ:], lse_tile)
        lse_ref[0, pl.ds(r0, A_TQ), :] = lse_tile
        return carry

    lax.fori_loop(0, n_sub, sub_tile, 0)


def _band_attention(qkv, window, dil):
    b, s, _ = qkv.shape
    half = window // (2 * dil)
    ls = s // dil
    assert half == A_TQ // 2 and A_WIN == A_TQ + 2 * half and ls % A_TQ == 0 and ls >= A_WIN
    rows_per_step = min(ls, 512)
    view = qkv.reshape(b, ls, dil * QKV_WIDTH)
    blocks_per_phase = QKV_WIDTH // A_WIDTH
    kern = functools.partial(_band_kernel, half=half, ls=ls, rows_per_step=rows_per_step)
    o, lse = pl.pallas_call(
        kern,
        grid=(b, dil, ls // rows_per_step),
        in_specs=[
            pl.BlockSpec((1, rows_per_step, A_WIDTH), lambda i, r, t: (i, t, r * blocks_per_phase)),
            pl.BlockSpec((1, ls, A_WIDTH), lambda i, r, t: (i, 0, r * blocks_per_phase + 1)),
            pl.BlockSpec((1, ls, A_WIDTH), lambda i, r, t: (i, 0, r * blocks_per_phase + 2)),
        ],
        out_specs=[
            pl.BlockSpec((1, rows_per_step, A_WIDTH), lambda i, r, t: (i, t, r)),
            pl.BlockSpec((1, rows_per_step, LANES), lambda i, r, t: (i, t, r)),
        ],
        out_shape=[
            jax.ShapeDtypeStruct((b, ls, dil * A_WIDTH), BF16),
            jax.ShapeDtypeStruct((b, ls, dil * LANES), F32),
        ],
        compiler_params=pltpu.CompilerParams(
            dimension_semantics=("arbitrary", "arbitrary", "arbitrary"),
            vmem_limit_bytes=VMEM_LIMIT_BYTES),
        name=f"band_attn_d{dil}",
    )(view, view, view)
    return o.reshape(b * s, A_WIDTH), lse.reshape(b * s, LANES)


B_TQ = 256
B_TK = 512


def _diff_kernel(q_ref, k_ref, v_ref, lam_ref, g_ref, o_ref, *, seq):
    lam_p = lam_ref[...]
    lam = (jnp.exp(jnp.sum(lam_p[0:1] * lam_p[1:2], axis=-1, keepdims=True))
           - jnp.exp(jnp.sum(lam_p[2:3] * lam_p[3:4], axis=-1, keepdims=True)) + LAMBDA_INIT)
    g = g_ref[...]
    n_chunks = seq // B_TK
    for h in range(B_HEADS):
        cs = slice(h * LANES, (h + 1) * LANES)
        qq = _split_heads_rows(q_ref[0, :, cs])

        def chunk(c, carry):
            m, l, acc = carry
            k0 = pl.multiple_of(c * B_TK, B_TK)
            kc = k_ref[0, pl.ds(k0, B_TK), cs]
            vc = v_ref[0, pl.ds(k0, B_TK), cs]
            s = lax.dot_general(qq, kc, (((1,), (1,)), ((), ())), preferred_element_type=F32)
            m_new = jnp.maximum(m, jnp.max(s, axis=-1, keepdims=True))
            alpha = jnp.exp(m - m_new)
            p = jnp.exp(s - m_new)
            l = alpha * l + jnp.sum(p, axis=-1, keepdims=True)
            acc = alpha * acc + jnp.dot(p.astype(BF16), vc, preferred_element_type=F32)
            return m_new, l, acc

        init = (jnp.full((2 * B_TQ, 1), MASK_VALUE, F32), jnp.zeros((2 * B_TQ, 1), F32),
                jnp.zeros((2 * B_TQ, LANES), F32))
        _, l, acc = lax.fori_loop(0, n_chunks, chunk, init)
        o = acc * (1.0 / l)
        o = o[:B_TQ] - lam * o[B_TQ:]
        o_ref[0, :, cs] = (_rms_norm(o, g, SUBLN_EPS) * (1.0 - LAMBDA_INIT)).astype(BF16)


def _diff_attention(qkv, lam_params, subln_g):
    b, s, _ = qkv.shape
    blk = lambda c: (lambda i, t: (i, 0, c))
    return pl.pallas_call(
        functools.partial(_diff_kernel, seq=s),
        grid=(b, s // B_TQ),
        in_specs=[
            pl.BlockSpec((1, B_TQ, B_WIDTH), lambda i, t: (i, t, 3)),
            pl.BlockSpec((1, s, B_WIDTH), blk(4)),
            pl.BlockSpec((1, s, B_WIDTH), blk(5)),
            pl.BlockSpec((4, HEAD_DIM), lambda i, t: (0, 0)),
            pl.BlockSpec((1, 2 * HEAD_DIM), lambda i, t: (0, 0)),
        ],
        out_specs=pl.BlockSpec((1, B_TQ, B_WIDTH), lambda i, t: (i, t, 0)),
        out_shape=jax.ShapeDtypeStruct((b, s, B_WIDTH), BF16),
        compiler_params=pltpu.CompilerParams(
            dimension_semantics=("arbitrary", "arbitrary"), vmem_limit_bytes=VMEM_LIMIT_BYTES),
        name="diff_attn",
    )(qkv, qkv, qkv, lam_params, subln_g).reshape(b * s, B_WIDTH)


MERGE_TM = 512
MERGE_TN = 512


def _merge_kernel(x_ref, g_ref, oa1_ref, oa2_ref, oa3_ref, l1_ref, l2_ref, l3_ref, ob_ref,
                  wg_ref, wa_ref, wb_ref, wo_ref, out_ref):
    x = x_ref[...]
    d = x.shape[-1]
    hb = _rms_norm(x, g_ref[...], NORM_EPS).astype(BF16)

    lses = [l1_ref[...], l2_ref[...], l3_ref[...]]
    m = jnp.maximum(jnp.maximum(lses[0], lses[1]), lses[2])
    es = [jnp.exp(v - m) for v in lses]
    inv = 1.0 / (es[0] + es[1] + es[2])
    ws = [e * inv for e in es]
    oas = [oa1_ref, oa2_ref, oa3_ref]
    lane = lax.broadcasted_iota(jnp.int32, (x.shape[0], LANES), 1)
    groups = []
    for j in range(A_WIDTH // LANES):
        acc = None
        for w, oa in zip(ws, oas):
            wj = jnp.where(lane < HEAD_DIM, w[:, 2 * j:2 * j + 1], w[:, 2 * j + 1:2 * j + 2])
            term = wj * oa[:, j * LANES:(j + 1) * LANES].astype(F32)
            acc = term if acc is None else acc + term
        groups.append(acc)
    attn_a = jnp.concatenate(groups, axis=-1).astype(BF16)
    attn_b = ob_ref[...]

    out = x
    for c in range(d // MERGE_TN):
        cs = slice(c * MERGE_TN, (c + 1) * MERGE_TN)
        ya = jnp.dot(attn_a, wa_ref[:, cs], preferred_element_type=F32)
        yb = jnp.dot(attn_b, wb_ref[:, cs], preferred_element_type=F32)
        ga = jnp.dot(hb, wg_ref[:, cs], preferred_element_type=F32)
        gb = jnp.dot(hb, wg_ref[:, d + c * MERGE_TN:d + (c + 1) * MERGE_TN], preferred_element_type=F32)
        merged = jax.nn.sigmoid(ga) * ya + jax.nn.sigmoid(gb) * yb
        out = out + jnp.dot(merged.astype(BF16), wo_ref[cs, :], preferred_element_type=F32)
    out_ref[...] = out


def _merge(x2d, g, oas, lses, ob, wg, wa, wb, wo):
    t, d = x2d.shape
    tm = MERGE_TM
    row = lambda w: pl.BlockSpec((tm, w), lambda i: (i, 0))
    full = lambda a: pl.BlockSpec(a.shape, lambda i: (0, 0))
    return pl.pallas_call(
        _merge_kernel,
        grid=(t // tm,),
        in_specs=[row(d), full(g)] + [row(A_WIDTH)] * 3 + [row(LANES)] * 3 + [row(B_WIDTH)]
        + [full(wg), full(wa), full(wb), full(wo)],
        out_specs=row(d),
        out_shape=jax.ShapeDtypeStruct((t, d), F32),
        compiler_params=pltpu.CompilerParams(
            dimension_semantics=("arbitrary",), vmem_limit_bytes=VMEM_LIMIT_BYTES),
        name="merge_proj",
    )(x2d, g, *oas, *lses, ob, wg, wa, wb, wo)


MLP_TM = 512
MLP_TF = 512


def _mlp_kernel(x_ref, g_ref, w1_ref, w2_ref, gf_ref, out_ref):
    x = x_ref[...]
    hb = _rms_norm(x, g_ref[...], NORM_EPS).astype(BF16)
    acc = x
    for c in range(w1_ref.shape[1] // MLP_TF):
        cs = slice(c * MLP_TF, (c + 1) * MLP_TF)
        u = jnp.maximum(jnp.dot(hb, w1_ref[:, cs], preferred_element_type=F32), 0.0)
        acc = acc + jnp.dot((u * u).astype(BF16), w2_ref[cs, :], preferred_element_type=F32)
    out_ref[...] = _rms_norm(acc, gf_ref[...], NORM_EPS)


def _mlp(x2d, g, w1, w2, gf):
    t, d = x2d.shape
    tm = MLP_TM
    row = pl.BlockSpec((tm, d), lambda i: (i, 0))
    full = lambda a: pl.BlockSpec(a.shape, lambda i: (0, 0))
    return pl.pallas_call(
        _mlp_kernel,
        grid=(t // tm,),
        in_specs=[row, full(g), full(w1), full(w2), full(gf)],
        out_specs=row,
        out_shape=jax.ShapeDtypeStruct((t, d), F32),
        compiler_params=pltpu.CompilerParams(
            dimension_semantics=("arbitrary",), vmem_limit_bytes=VMEM_LIMIT_BYTES),
        name="mlp_norm",
    )(x2d, g, w1, w2, gf)


def _rope_tables(seq):
    pos = jnp.arange(seq, dtype=F32)
    inv_freq = ROPE_THETA ** (-jnp.arange(0, HEAD_DIM, 2, dtype=F32) / HEAD_DIM)
    ang = pos[:, None] * inv_freq[None, :]
    cos = jnp.cos(ang)
    sin = jnp.sin(ang)
    cos = jnp.concatenate([cos, cos, cos, cos], axis=-1)
    sin = jnp.concatenate([-sin, sin, -sin, sin], axis=-1)
    return cos, sin


def kernel(x, w_in, w_branch_a, w_branch_b, w_out, lambda_q1, lambda_k1, lambda_q2, lambda_k2,
           diff_subln_g, norm_mix_g, norm_mlp_g, w_ff1, w_ff2, norm_final_g):
    b, s, d = x.shape
    assert w_in.shape[0] == 1, "single-layer block: the final norm is fused into the MLP kernel"
    cos, sin = _rope_tables(s)
    x2d = x.reshape(b * s, d)
    w_in_l = w_in[0].astype(BF16)
    g_mix = norm_mix_g[0].reshape(1, d)
    qkv = _qkv_proj(x, g_mix, w_in_l[:, :QKV_WIDTH], cos, sin, ts=512)
    oas, lses = zip(*[_band_attention(qkv, w, dil) for w, dil in DILATED_PATTERNS])
    lam_params = jnp.stack([lambda_q1[0], lambda_k1[0], lambda_q2[0], lambda_k2[0]]).astype(F32)
    ob = _diff_attention(qkv, lam_params, diff_subln_g[0].reshape(1, 2 * HEAD_DIM).astype(F32))
    x2d = _merge(x2d, g_mix, oas, lses, ob, w_in_l[:, QKV_WIDTH:], w_branch_a[0].astype(BF16),
                 w_branch_b[0].astype(BF16), w_out[0].astype(BF16))
    x2d = _mlp(x2d, norm_mlp_g[0].reshape(1, d), w_ff1[0].astype(BF16), w_ff2[0].astype(BF16),
               norm_final_g.reshape(1, d))
    return x2d.reshape(b, s, d)
```

```python
import functools
import math

import numpy as np
import jax
import jax.numpy as jnp
from jax import lax
from jax.experimental import pallas as pl
from jax.experimental.pallas import tpu as pltpu

HEAD_DIM = 64
HALF_DIM = HEAD_DIM // 2
LANES = 128
A_HEADS = 8
A_WIDTH = A_HEADS * HEAD_DIM
B_HEADS = 4
B_WIDTH = B_HEADS * 2 * HEAD_DIM
QKV_A_WIDTH = 3 * A_WIDTH
QK_B_WIDTH = 2 * B_WIDTH
QKV_WIDTH = QKV_A_WIDTH + QK_B_WIDTH + B_WIDTH
DILATED_PATTERNS = ((128, 1), (512, 4), (2048, 16))
ROPE_THETA = 10000.0
NORM_EPS = 1e-6
SUBLN_EPS = 1e-5
LAMBDA_INIT = 0.8 - 0.6 * math.exp(-0.3 * 0)
QK_SCALE = 1.0 / math.sqrt(HEAD_DIM)
MASK_VALUE = -1e30
VMEM_LIMIT_BYTES = 56 * 1024 * 1024
PERM = 256
TOKEN_TILE = 512

BF16 = jnp.bfloat16
F32 = jnp.float32
NT_DIMS = (((1,), (1,)), ((), ()))


def _rms_norm(x, g, eps):
    return (x * lax.rsqrt(jnp.mean(x * x, axis=-1, keepdims=True) + eps)) * g


def _split_heads_rows(q):
    lane = lax.broadcasted_iota(jnp.int32, q.shape, 1)
    zero = jnp.zeros_like(q)
    return jnp.concatenate(
        [jnp.where(lane < HEAD_DIM, q, zero), jnp.where(lane >= HEAD_DIM, q, zero)], axis=0)


def _phase_perm(dil):
    n = PERM // dil
    p = np.zeros((PERM, PERM), np.float32)
    for r in range(dil):
        for i in range(n):
            p[r * n + i, i * dil + r] = 1.0
    return p


def _qkv_kernel(x_ref, g_ref, w_ref, wvt_ref, cos_ref, sin_ref, p4_ref, p16_ref,
                qkva_ref, qkb_ref, vbt_ref, ph4_ref, ph16_ref):
    x = x_ref[0]
    ts = x.shape[0]
    hb = _rms_norm(x, g_ref[...], NORM_EPS).astype(BF16)
    cos = cos_ref[...]
    sin = sin_ref[...]
    lane = lax.broadcasted_iota(jnp.int32, cos.shape, 1)
    first_half = (lane % HEAD_DIM) < HALF_DIM

    def rope(t, scale):
        parts = []
        for j in range(t.shape[1] // LANES):
            tj = t[:, j * LANES:(j + 1) * LANES]
            rot = jnp.where(first_half, pltpu.roll(tj, LANES - HALF_DIM, 1), pltpu.roll(tj, HALF_DIM, 1))
            tj = tj * cos + rot * sin
            parts.append((tj * scale if scale != 1.0 else tj).astype(BF16))
        return parts

    for c in range(5):
        t = jnp.dot(hb, w_ref[:, c * A_WIDTH:(c + 1) * A_WIDTH], preferred_element_type=F32)
        dst, base = (qkva_ref, c * A_WIDTH) if c < 3 else (qkb_ref, (c - 3) * A_WIDTH)
        if c == 2:
            dst[0, :, base:base + A_WIDTH] = t.astype(BF16)
        else:
            for j, tj in enumerate(rope(t, QK_SCALE if c in (0, 3) else 1.0)):
                dst[0, :, base + j * LANES:base + (j + 1) * LANES] = tj
    vbt_ref[0] = lax.dot_general(wvt_ref[...], hb, NT_DIMS, preferred_element_type=F32).astype(BF16)

    for dil, p_ref, ph_ref in ((4, p4_ref, ph4_ref), (16, p16_ref, ph16_ref)):
        n = PERM // dil
        for u in range(ts // PERM):
            for c in range(3):
                cs = slice(c * A_WIDTH, (c + 1) * A_WIDTH)
                tb = qkva_ref[0, u * PERM:(u + 1) * PERM, cs]
                ph = jnp.dot(p_ref[...], tb, preferred_element_type=F32).astype(BF16)
                for r in range(dil):
                    ph_ref[0, r, u * n:(u + 1) * n, cs] = ph[r * n:(r + 1) * n]


def _qkv_proj(x, g, w_qkv, w_vbt, cos, sin, p4, p16):
    b, s, d = x.shape
    ts = TOKEN_TILE
    const = lambda a: pl.BlockSpec(a.shape, lambda i, j: (0,) * a.ndim)
    return pl.pallas_call(
        _qkv_kernel,
        grid=(b, s // ts),
        in_specs=[
            pl.BlockSpec((1, ts, d), lambda i, j: (i, j, 0)),
            const(g), const(w_qkv), const(w_vbt),
            pl.BlockSpec((ts, LANES), lambda i, j: (j, 0)),
            pl.BlockSpec((ts, LANES), lambda i, j: (j, 0)),
            const(p4), const(p16),
        ],
        out_specs=[
            pl.BlockSpec((1, ts, QKV_A_WIDTH), lambda i, j: (i, j, 0)),
            pl.BlockSpec((1, ts, QK_B_WIDTH), lambda i, j: (i, j, 0)),
            pl.BlockSpec((1, B_WIDTH, ts), lambda i, j: (i, 0, j)),
            pl.BlockSpec((1, 4, ts // 4, QKV_A_WIDTH), lambda i, j: (i, 0, j, 0)),
            pl.BlockSpec((1, 16, ts // 16, QKV_A_WIDTH), lambda i, j: (i, 0, j, 0)),
        ],
        out_shape=[
            jax.ShapeDtypeStruct((b, s, QKV_A_WIDTH), BF16),
            jax.ShapeDtypeStruct((b, s, QK_B_WIDTH), BF16),
            jax.ShapeDtypeStruct((b, B_WIDTH, s), BF16),
            jax.ShapeDtypeStruct((b, 4, s // 4, QKV_A_WIDTH), BF16),
            jax.ShapeDtypeStruct((b, 16, s // 16, QKV_A_WIDTH), BF16),
        ],
        compiler_params=pltpu.CompilerParams(
            dimension_semantics=("arbitrary", "arbitrary"), vmem_limit_bytes=VMEM_LIMIT_BYTES),
        name="qkv_proj",
    )(x, g, w_qkv, w_vbt, cos, sin, p4, p16)


A_TQ = 128
A_WIN = 256


def _band_kernel(q_ref, k_ref, v_ref, o_ref, lse_ref, *, half, ls, rows_per_step):
    step = pl.program_id(2)
    n_sub = rows_per_step // A_TQ
    row = lax.broadcasted_iota(jnp.int32, (2 * A_TQ, A_WIN), 0)
    col = lax.broadcasted_iota(jnp.int32, (2 * A_TQ, A_WIN), 1)
    rel0 = col - (row % A_TQ)
    out_lane = lax.broadcasted_iota(jnp.int32, (A_TQ, LANES), 1)

    def sub_tile(i, carry):
        r0 = pl.multiple_of(i * A_TQ, A_TQ)
        t0 = step * rows_per_step + r0
        start = pl.multiple_of(jnp.clip(t0 - half, 0, ls - A_WIN), half)
        bias = jnp.where(jnp.abs(rel0 + (start - t0)) <= half, 0.0, MASK_VALUE).astype(F32)
        lse_tile = jnp.zeros((A_TQ, LANES), F32)
        for j in range(A_WIDTH // LANES):
            cs = slice(j * LANES, (j + 1) * LANES)
            qq = _split_heads_rows(q_ref[0, 0, pl.ds(r0, A_TQ), cs])
            kw = k_ref[0, 0, pl.ds(start, A_WIN), cs]
            vw = v_ref[0, 0, pl.ds(start, A_WIN), cs]
            s = lax.dot_general(qq, kw, NT_DIMS, preferred_element_type=F32) + bias
            m = jnp.max(s, axis=-1, keepdims=True)
            p = jnp.exp(s - m)
            l = jnp.sum(p, axis=-1, keepdims=True)
            pv = jnp.dot(p.astype(BF16), vw, preferred_element_type=F32) * (1.0 / l)
            o_ref[0, 0, pl.ds(r0, A_TQ), cs] = jnp.where(
                out_lane < HEAD_DIM, pv[:A_TQ], pv[A_TQ:]).astype(BF16)
            lse = m + jnp.log(l)
            lse_tile = jnp.where(out_lane == 2 * j, lse[:A_TQ], lse_tile)
            lse_tile = jnp.where(out_lane == 2 * j + 1, lse[A_TQ:], lse_tile)
        lse_ref[0, 0, pl.ds(r0, A_TQ), :] = lse_tile
        return carry

    lax.fori_loop(0, n_sub, sub_tile, 0)


def _band_attention(ph, window):
    b, dil, ls, _ = ph.shape
    half = window // (2 * dil)
    assert half == A_TQ // 2 and A_WIN == A_TQ + 2 * half and ls % A_TQ == 0 and ls >= A_WIN
    rows_per_step = min(ls, 512)
    kern = functools.partial(_band_kernel, half=half, ls=ls, rows_per_step=rows_per_step)
    return pl.pallas_call(
        kern,
        grid=(b, dil, ls // rows_per_step),
        in_specs=[
            pl.BlockSpec((1, 1, rows_per_step, A_WIDTH), lambda i, r, t: (i, r, t, 0)),
            pl.BlockSpec((1, 1, ls, A_WIDTH), lambda i, r, t: (i, r, 0, 1)),
            pl.BlockSpec((1, 1, ls, A_WIDTH), lambda i, r, t: (i, r, 0, 2)),
        ],
        out_specs=[
            pl.BlockSpec((1, 1, rows_per_step, A_WIDTH), lambda i, r, t: (i, r, t, 0)),
            pl.BlockSpec((1, 1, rows_per_step, LANES), lambda i, r, t: (i, r, t, 0)),
        ],
        out_shape=[
            jax.ShapeDtypeStruct((b, dil, ls, A_WIDTH), BF16),
            jax.ShapeDtypeStruct((b, dil, ls, LANES), F32),
        ],
        compiler_params=pltpu.CompilerParams(
            dimension_semantics=("arbitrary", "arbitrary", "arbitrary"),
            vmem_limit_bytes=VMEM_LIMIT_BYTES),
        name=f"band_attn_d{dil}",
    )(ph, ph, ph)


B_TQ = 256
B_KB = 1024
B_TK = 256


def _diff_kernel(q_ref, k_ref, vt_ref, lam_ref, g_ref, o_ref, *, seq):
    lam_p = lam_ref[...]
    lam = (jnp.exp(jnp.sum(lam_p[0:1] * lam_p[1:2], axis=-1, keepdims=True))
           - jnp.exp(jnp.sum(lam_p[2:3] * lam_p[3:4], axis=-1, keepdims=True)) + LAMBDA_INIT)
    g = g_ref[...]

    def q_tile(t, carry):
        q0 = pl.multiple_of(t * B_TQ, B_TQ)
        qq = _split_heads_rows(q_ref[0, pl.ds(q0, B_TQ), :])
        m = jnp.full((1, 2 * B_TQ), MASK_VALUE, F32)
        l = jnp.zeros((1, 2 * B_TQ), F32)
        acc = jnp.zeros((2 * HEAD_DIM, 2 * B_TQ), F32)
        for kb in range(seq // B_KB):
            s = lax.dot_general(k_ref[0, kb * B_KB:(kb + 1) * B_KB, :], qq, NT_DIMS,
                                preferred_element_type=F32)
            m_new = jnp.maximum(m, jnp.max(s, axis=0, keepdims=True))
            alpha = jnp.exp(m - m_new)
            m = m_new
            l = alpha * l
            acc = alpha * acc
            for c in range(B_KB // B_TK):
                p = jnp.exp(s[c * B_TK:(c + 1) * B_TK] - m)
                l = l + jnp.sum(p, axis=0, keepdims=True)
                k0 = kb * B_KB + c * B_TK
                acc = acc + jnp.dot(vt_ref[0, :, k0:k0 + B_TK], p.astype(BF16),
                                    preferred_element_type=F32)
        o = acc * (1.0 / l)
        o = o[:, :B_TQ] - lam * o[:, B_TQ:]
        o = o * lax.rsqrt(jnp.mean(o * o, axis=0, keepdims=True) + SUBLN_EPS)
        o_ref[0, pl.ds(q0, B_TQ), :] = ((o.T * g) * (1.0 - LAMBDA_INIT)).astype(BF16)
        return carry

    lax.fori_loop(0, seq // B_TQ, q_tile, 0)


def _diff_attention(qkb, vbt, lam_params, subln_g):
    b, s, _ = qkb.shape
    return pl.pallas_call(
        functools.partial(_diff_kernel, seq=s),
        grid=(b, B_HEADS),
        in_specs=[
            pl.BlockSpec((1, s, LANES), lambda i, h: (i, 0, h)),
            pl.BlockSpec((1, s, LANES), lambda i, h: (i, 0, B_HEADS + h)),
            pl.BlockSpec((1, LANES, s), lambda i, h: (i, h, 0)),
            pl.BlockSpec((4, HEAD_DIM), lambda i, h: (0, 0)),
            pl.BlockSpec((1, 2 * HEAD_DIM), lambda i, h: (0, 0)),
        ],
        out_specs=pl.BlockSpec((1, s, LANES), lambda i, h: (i, 0, h)),
        out_shape=jax.ShapeDtypeStruct((b, s, B_WIDTH), BF16),
        compiler_params=pltpu.CompilerParams(
            dimension_semantics=("arbitrary", "arbitrary"), vmem_limit_bytes=VMEM_LIMIT_BYTES),
        name="diff_attn",
    )(qkb, qkb, vbt, lam_params, subln_g)


MERGE_TN = 512


def _split3_bf16(v):
    hi = v.astype(BF16)
    r1 = v - hi.astype(F32)
    mid = r1.astype(BF16)
    lo = (r1 - mid.astype(F32)).astype(BF16)
    return hi, mid, lo


def _merge_kernel(x_ref, g_ref, o1_ref, l1_ref, o4_ref, l4_ref, o16_ref, l16_ref, ob_ref,
                  p4_ref, p16_ref, wg_ref, wa_ref, wb_ref, wo_ref, out_ref):
    x = x_ref[0]
    tm, d = x.shape
    hb = _rms_norm(x, g_ref[...], NORM_EPS).astype(BF16)

    def token_order(o_ref_, l_ref_, p_ref_, dil):
        n = PERM // dil
        os_, ls_ = [], []
        for u in range(tm // PERM):
            xo = jnp.concatenate([o_ref_[0, r, u * n:(u + 1) * n, :] for r in range(dil)], axis=0)
            os_.append(jnp.dot(p_ref_[...], xo, preferred_element_type=F32))
            xl = jnp.concatenate([l_ref_[0, r, u * n:(u + 1) * n, :] for r in range(dil)], axis=0)
            ls_.append(sum(jnp.dot(p_ref_[...], part, preferred_element_type=F32)
                           for part in _split3_bf16(xl)))
        return jnp.concatenate(os_, axis=0), jnp.concatenate(ls_, axis=0)

    o4, l4 = token_order(o4_ref, l4_ref, p4_ref, 4)
    o16, l16 = token_order(o16_ref, l16_ref, p16_ref, 16)
    outs = [o1_ref[0, 0].astype(F32), o4, o16]
    lses = [l1_ref[0, 0], l4, l16]

    m = jnp.maximum(jnp.maximum(lses[0], lses[1]), lses[2])
    es = [jnp.exp(v - m) for v in lses]
    inv = 1.0 / (es[0] + es[1] + es[2])
    ws = [e * inv for e in es]
    lane = lax.broadcasted_iota(jnp.int32, (tm, LANES), 1)
    groups = []
    for j in range(A_WIDTH // LANES):
        acc = None
        for w, o in zip(ws, outs):
            wj = jnp.where(lane < HEAD_DIM, w[:, 2 * j:2 * j + 1], w[:, 2 * j + 1:2 * j + 2])
            term = wj * o[:, j * LANES:(j + 1) * LANES]
            acc = term if acc is None else acc + term
        groups.append(acc)
    attn_a = jnp.concatenate(groups, axis=-1).astype(BF16)
    attn_b = ob_ref[0]

    out = x
    for c in range(d // MERGE_TN):
        cs = slice(c * MERGE_TN, (c + 1) * MERGE_TN)
        ya = jnp.dot(attn_a, wa_ref[:, cs], preferred_element_type=F32)
        yb = jnp.dot(attn_b, wb_ref[:, cs], preferred_element_type=F32)
        ga = jnp.dot(hb, wg_ref[:, cs], preferred_element_type=F32)
        gb = jnp.dot(hb, wg_ref[:, d + c * MERGE_TN:d + (c + 1) * MERGE_TN], preferred_element_type=F32)
        merged = jax.nn.sigmoid(ga) * ya + jax.nn.sigmoid(gb) * yb
        out = out + jnp.dot(merged.astype(BF16), wo_ref[cs, :], preferred_element_type=F32)
    out_ref[0] = out


def _merge(x, g, o1, l1, o4, l4, o16, l16, ob, p4, p16, wg, wa, wb, wo):
    b, s, d = x.shape
    tm = TOKEN_TILE
    tok = lambda w: pl.BlockSpec((1, tm, w), lambda i, j: (i, j, 0))
    phase = lambda dil, w: pl.BlockSpec((1, dil, tm // dil, w), lambda i, j: (i, 0, j, 0))
    const = lambda a: pl.BlockSpec(a.shape, lambda i, j: (0,) * a.ndim)
    return pl.pallas_call(
        _merge_kernel,
        grid=(b, s // tm),
        in_specs=[tok(d), const(g), phase(1, A_WIDTH), phase(1, LANES), phase(4, A_WIDTH), phase(4, LANES),
                  phase(16, A_WIDTH), phase(16, LANES), tok(B_WIDTH), const(p4), const(p16),
                  const(wg), const(wa), const(wb), const(wo)],
        out_specs=tok(d),
        out_shape=jax.ShapeDtypeStruct((b, s, d), F32),
        compiler_params=pltpu.CompilerParams(
            dimension_semantics=("arbitrary", "arbitrary"), vmem_limit_bytes=VMEM_LIMIT_BYTES),
        name="merge_proj",
    )(x, g, o1, l1, o4, l4, o16, l16, ob, p4, p16, wg, wa, wb, wo)


MLP_TF = 512


def _mlp_kernel(x_ref, g_ref, w1_ref, w2_ref, gf_ref, out_ref):
    x = x_ref[...]
    hb = _rms_norm(x, g_ref[...], NORM_EPS).astype(BF16)
    acc = x
    for c in range(w1_ref.shape[1] // MLP_TF):
        cs = slice(c * MLP_TF, (c + 1) * MLP_TF)
        u = jnp.maximum(jnp.dot(hb, w1_ref[:, cs], preferred_element_type=F32), 0.0)
        acc = acc + jnp.dot((u * u).astype(BF16), w2_ref[cs, :], preferred_element_type=F32)
    out_ref[...] = _rms_norm(acc, gf_ref[...], NORM_EPS)


def _mlp(x2d, g, w1, w2, gf):
    t, d = x2d.shape
    tm = TOKEN_TILE
    row = pl.BlockSpec((tm, d), lambda i: (i, 0))
    full = lambda a: pl.BlockSpec(a.shape, lambda i: (0, 0))
    return pl.pallas_call(
        _mlp_kernel,
        grid=(t // tm,),
        in_specs=[row, full(g), full(w1), full(w2), full(gf)],
        out_specs=row,
        out_shape=jax.ShapeDtypeStruct((t, d), F32),
        compiler_params=pltpu.CompilerParams(
            dimension_semantics=("arbitrary",), vmem_limit_bytes=VMEM_LIMIT_BYTES),
        name="mlp_norm",
    )(x2d, g, w1, w2, gf)


def _rope_tables(seq):
    pos = jnp.arange(seq, dtype=F32)
    inv_freq = ROPE_THETA ** (-jnp.arange(0, HEAD_DIM, 2, dtype=F32) / HEAD_DIM)
    ang = pos[:, None] * inv_freq[None, :]
    cos = jnp.cos(ang)
    sin = jnp.sin(ang)
    cos = jnp.concatenate([cos, cos, cos, cos], axis=-1)
    sin = jnp.concatenate([-sin, sin, -sin, sin], axis=-1)
    return cos, sin


def kernel(x, w_in, w_branch_a, w_branch_b, w_out, lambda_q1, lambda_k1, lambda_q2, lambda_k2,
           diff_subln_g, norm_mix_g, norm_mlp_g, w_ff1, w_ff2, norm_final_g):
    b, s, d = x.shape
    assert w_in.shape[0] == 1, "single-layer block: the final norm is fused into the MLP kernel"
    assert s % B_KB == 0 and s % TOKEN_TILE == 0
    cos, sin = _rope_tables(s)
    w_in_l = w_in[0].astype(BF16)
    g_mix = norm_mix_g[0].reshape(1, d)
    vb_lo = QKV_A_WIDTH + QK_B_WIDTH
    p4 = jnp.asarray(_phase_perm(4), BF16)
    p16 = jnp.asarray(_phase_perm(16), BF16)
    qkva, qkb, vbt, ph4, ph16 = _qkv_proj(
        x, g_mix, w_in_l[:, :vb_lo], w_in_l[:, vb_lo:QKV_WIDTH].T, cos, sin, p4, p16)
    o1, l1 = _band_attention(qkva.reshape(b, 1, s, QKV_A_WIDTH), DILATED_PATTERNS[0][0])
    o4, l4 = _band_attention(ph4, DILATED_PATTERNS[1][0])
    o16, l16 = _band_attention(ph16, DILATED_PATTERNS[2][0])
    lam_params = jnp.stack([lambda_q1[0], lambda_k1[0], lambda_q2[0], lambda_k2[0]]).astype(F32)
    ob = _diff_attention(qkb, vbt, lam_params, diff_subln_g[0].reshape(1, 2 * HEAD_DIM).astype(F32))
    x1 = _merge(x, g_mix, o1, l1, o4, l4, o16, l16, ob, p4.T, p16.T, w_in_l[:, QKV_WIDTH:],
                w_branch_a[0].astype(BF16), w_branch_b[0].astype(BF16), w_out[0].astype(BF16))
    out = _mlp(x1.reshape(b * s, d), norm_mlp_g[0].reshape(1, d), w_ff1[0].astype(BF16),
               w_ff2[0].astype(BF16), norm_final_g.reshape(1, d))
    return out.reshape(b, s, d)
```

```python
import functools
import math

import numpy as np
import jax
import jax.numpy as jnp
from jax import lax
from jax.experimental import pallas as pl
from jax.experimental.pallas import tpu as pltpu

HEAD_DIM = 64
HALF_DIM = HEAD_DIM // 2
LANES = 128
A_HEADS = 8
A_WIDTH = A_HEADS * HEAD_DIM
B_HEADS = 4
B_WIDTH = B_HEADS * 2 * HEAD_DIM
QKV_A_WIDTH = 3 * A_WIDTH
QK_B_WIDTH = 2 * B_WIDTH
QKV_WIDTH = QKV_A_WIDTH + QK_B_WIDTH + B_WIDTH
DILATED_PATTERNS = ((128, 1), (512, 4), (2048, 16))
ROPE_THETA = 10000.0
NORM_EPS = 1e-6
SUBLN_EPS = 1e-5
LAMBDA_INIT = 0.8 - 0.6 * math.exp(-0.3 * 0)
QK_SCALE_LOG2E = math.log2(math.e) / math.sqrt(HEAD_DIM)
LN2 = math.log(2.0)
MASK_VALUE = -1e30
VMEM_LIMIT_BYTES = 56 * 1024 * 1024
PERM = 256
TOKEN_TILE = 512

BF16 = jnp.bfloat16
F32 = jnp.float32
NT_DIMS = (((1,), (1,)), ((), ()))


def _rms_norm(x, g, eps):
    return (x * lax.rsqrt(jnp.mean(x * x, axis=-1, keepdims=True) + eps)) * g


def _split_heads_rows(q):
    lane = lax.broadcasted_iota(jnp.int32, q.shape, 1)
    zero = jnp.zeros_like(q)
    return jnp.concatenate(
        [jnp.where(lane < HEAD_DIM, q, zero), jnp.where(lane >= HEAD_DIM, q, zero)], axis=0)


def _phase_perm(dil):
    n = PERM // dil
    p = np.zeros((PERM, PERM), np.float32)
    for r in range(dil):
        for i in range(n):
            p[r * n + i, i * dil + r] = 1.0
    return p


def _qkv_kernel(x_ref, g_ref, w_ref, wvt_ref, cos_ref, sin_ref, p4_ref, p16_ref,
                qkva_ref, qkb_ref, vbt_ref, ph4_ref, ph16_ref):
    x = x_ref[0]
    ts = x.shape[0]
    hb = _rms_norm(x, g_ref[...], NORM_EPS).astype(BF16)
    cos = cos_ref[...]
    sin = sin_ref[...]
    lane = lax.broadcasted_iota(jnp.int32, cos.shape, 1)
    first_half = (lane % HEAD_DIM) < HALF_DIM

    def rope(t, scale):
        parts = []
        for j in range(t.shape[1] // LANES):
            tj = t[:, j * LANES:(j + 1) * LANES]
            rot = jnp.where(first_half, pltpu.roll(tj, LANES - HALF_DIM, 1), pltpu.roll(tj, HALF_DIM, 1))
            tj = tj * cos + rot * sin
            parts.append((tj * scale if scale != 1.0 else tj).astype(BF16))
        return parts

    for c in range(5):
        t = jnp.dot(hb, w_ref[:, c * A_WIDTH:(c + 1) * A_WIDTH], preferred_element_type=F32)
        dst, base = (qkva_ref, c * A_WIDTH) if c < 3 else (qkb_ref, (c - 3) * A_WIDTH)
        if c == 2:
            dst[0, :, base:base + A_WIDTH] = t.astype(BF16)
        else:
            scale = QK_SCALE_LOG2E if c in (0, 3) else 1.0
            for j, tj in enumerate(rope(t, scale)):
                dst[0, :, base + j * LANES:base + (j + 1) * LANES] = tj
    vbt = lax.dot_general(wvt_ref[...], hb, NT_DIMS, preferred_element_type=F32).astype(BF16)
    for u in range(ts // B_TK):
        vbt_ref[0, u] = vbt[:, u * B_TK:(u + 1) * B_TK]

    for dil, p_ref, ph_ref in ((4, p4_ref, ph4_ref), (16, p16_ref, ph16_ref)):
        n = PERM // dil
        for u in range(ts // PERM):
            for c in range(3):
                cs = slice(c * A_WIDTH, (c + 1) * A_WIDTH)
                tb = qkva_ref[0, u * PERM:(u + 1) * PERM, cs]
                ph = jnp.dot(p_ref[...], tb, preferred_element_type=F32).astype(BF16)
                for r in range(dil):
                    ph_ref[0, r, u * n:(u + 1) * n, cs] = ph[r * n:(r + 1) * n]


def _qkv_proj(x, g, w_qkv, w_vbt, cos, sin, p4, p16):
    b, s, d = x.shape
    ts = TOKEN_TILE
    const = lambda a: pl.BlockSpec(a.shape, lambda i, j: (0,) * a.ndim)
    return pl.pallas_call(
        _qkv_kernel,
        grid=(b, s // ts),
        in_specs=[
            pl.BlockSpec((1, ts, d), lambda i, j: (i, j, 0)),
            const(g), const(w_qkv), const(w_vbt),
            pl.BlockSpec((ts, LANES), lambda i, j: (j, 0)),
            pl.BlockSpec((ts, LANES), lambda i, j: (j, 0)),
            const(p4), const(p16),
        ],
        out_specs=[
            pl.BlockSpec((1, ts, QKV_A_WIDTH), lambda i, j: (i, j, 0)),
            pl.BlockSpec((1, ts, QK_B_WIDTH), lambda i, j: (i, j, 0)),
            pl.BlockSpec((1, ts // B_TK, B_WIDTH, B_TK), lambda i, j: (i, j, 0, 0)),
            pl.BlockSpec((1, 4, ts // 4, QKV_A_WIDTH), lambda i, j: (i, 0, j, 0)),
            pl.BlockSpec((1, 16, ts // 16, QKV_A_WIDTH), lambda i, j: (i, 0, j, 0)),
        ],
        out_shape=[
            jax.ShapeDtypeStruct((b, s, QKV_A_WIDTH), BF16),
            jax.ShapeDtypeStruct((b, s, QK_B_WIDTH), BF16),
            jax.ShapeDtypeStruct((b, s // B_TK, B_WIDTH, B_TK), BF16),
            jax.ShapeDtypeStruct((b, 4, s // 4, QKV_A_WIDTH), BF16),
            jax.ShapeDtypeStruct((b, 16, s // 16, QKV_A_WIDTH), BF16),
        ],
        compiler_params=pltpu.CompilerParams(
            dimension_semantics=("arbitrary", "arbitrary"), vmem_limit_bytes=VMEM_LIMIT_BYTES),
        name="qkv_proj",
    )(x, g, w_qkv, w_vbt, cos, sin, p4, p16)


A_TQ = 128
A_WIN = 256


def _band_kernel(q_ref, k_ref, v_ref, o_ref, lse_ref, *, half, ls, rows_per_step):
    step = pl.program_id(2)
    n_sub = rows_per_step // A_TQ
    row = lax.broadcasted_iota(jnp.int32, (2 * A_TQ, A_WIN), 0)
    col = lax.broadcasted_iota(jnp.int32, (2 * A_TQ, A_WIN), 1)
    rel0 = col - (row % A_TQ)
    out_lane = lax.broadcasted_iota(jnp.int32, (A_TQ, LANES), 1)

    def sub_tile(i, carry):
        r0 = pl.multiple_of(i * A_TQ, A_TQ)
        t0 = step * rows_per_step + r0
        start = pl.multiple_of(jnp.clip(t0 - half, 0, ls - A_WIN), half)
        bias = jnp.where(jnp.abs(rel0 + (start - t0)) <= half, 0.0, MASK_VALUE).astype(F32)
        lse_tile = jnp.zeros((A_TQ, LANES), F32)
        for j in range(A_WIDTH // LANES):
            cs = slice(j * LANES, (j + 1) * LANES)
            qq = _split_heads_rows(q_ref[0, 0, pl.ds(r0, A_TQ), cs])
            kw = k_ref[0, 0, pl.ds(start, A_WIN), cs]
            vw = v_ref[0, 0, pl.ds(start, A_WIN), cs]
            s = lax.dot_general(qq, kw, NT_DIMS, preferred_element_type=F32) + bias
            m = jnp.max(s, axis=-1, keepdims=True)
            p = jnp.exp2(s - m)
            l = jnp.sum(p, axis=-1, keepdims=True)
            pv = jnp.dot(p.astype(BF16), vw, preferred_element_type=F32) * (1.0 / l)
            o_ref[0, 0, pl.ds(r0, A_TQ), cs] = jnp.where(
                out_lane < HEAD_DIM, pv[:A_TQ], pv[A_TQ:]).astype(BF16)
            lse = (m + jnp.log2(l)) * LN2
            lse_tile = jnp.where(out_lane == 2 * j, lse[:A_TQ], lse_tile)
            lse_tile = jnp.where(out_lane == 2 * j + 1, lse[A_TQ:], lse_tile)
        lse_ref[0, 0, pl.ds(r0, A_TQ), :] = lse_tile
        return carry

    lax.fori_loop(0, n_sub, sub_tile, 0, unroll=4)


def _band_attention(ph, window):
    b, dil, ls, _ = ph.shape
    half = window // (2 * dil)
    assert half == A_TQ // 2 and A_WIN == A_TQ + 2 * half and ls % A_TQ == 0 and ls >= A_WIN
    rows_per_step = min(ls, 512)
    kern = functools.partial(_band_kernel, half=half, ls=ls, rows_per_step=rows_per_step)
    return pl.pallas_call(
        kern,
        grid=(b, dil, ls // rows_per_step),
        in_specs=[
            pl.BlockSpec((1, 1, rows_per_step, A_WIDTH), lambda i, r, t: (i, r, t, 0)),
            pl.BlockSpec((1, 1, ls, A_WIDTH), lambda i, r, t: (i, r, 0, 1)),
            pl.BlockSpec((1, 1, ls, A_WIDTH), lambda i, r, t: (i, r, 0, 2)),
        ],
        out_specs=[
            pl.BlockSpec((1, 1, rows_per_step, A_WIDTH), lambda i, r, t: (i, r, t, 0)),
            pl.BlockSpec((1, 1, rows_per_step, LANES), lambda i, r, t: (i, r, t, 0)),
        ],
        out_shape=[
            jax.ShapeDtypeStruct((b, dil, ls, A_WIDTH), BF16),
            jax.ShapeDtypeStruct((b, dil, ls, LANES), F32),
        ],
        compiler_params=pltpu.CompilerParams(
            dimension_semantics=("arbitrary", "arbitrary", "arbitrary"),
            vmem_limit_bytes=VMEM_LIMIT_BYTES),
        name=f"band_attn_d{dil}",
    )(ph, ph, ph)


B_TQ = 256
B_KB = 512
B_TK = 256


def _diff_kernel(q_ref, k_ref, vt_ref, lam_ref, g_ref, o_ref, sa_ref, sb_ref, *, seq):
    lam_p = lam_ref[...]
    lam = (jnp.exp(jnp.sum(lam_p[0:1] * lam_p[1:2], axis=-1, keepdims=True))
           - jnp.exp(jnp.sum(lam_p[2:3] * lam_p[3:4], axis=-1, keepdims=True)) + LAMBDA_INIT)
    g = g_ref[...]

    n_kb = seq // B_KB
    n_qt = seq // B_TQ
    bufs = (sa_ref, sb_ref)

    def load_qq(t):
        return _split_heads_rows(q_ref[0, pl.ds(pl.multiple_of(t * B_TQ, B_TQ), B_TQ), :])

    def scores(qq, kb, s_ref):
        s = lax.dot_general(k_ref[0, kb * B_KB:(kb + 1) * B_KB, :], qq, NT_DIMS, preferred_element_type=F32)
        s_ref[...] = s
        return jnp.max(s, axis=0, keepdims=True)

    def softmax_pv(kb, s_ref, cmax, m, l, acc):
        m_new = jnp.maximum(m, cmax)
        alpha = jnp.exp2(m - m_new)
        l = alpha * l
        acc = alpha * acc
        for c in range(B_KB // B_TK):
            p = jnp.exp2(s_ref[c * B_TK:(c + 1) * B_TK, :] - m_new)
            l = l + jnp.sum(p, axis=0, keepdims=True)
            acc = acc + jnp.dot(vt_ref[0, kb * (B_KB // B_TK) + c], p.astype(BF16),
                                preferred_element_type=F32)
        return m_new, l, acc

    def q_tile(t, cmax):
        qq = load_qq(t)
        qq_next = load_qq(jnp.minimum(t + 1, n_qt - 1))
        m = jnp.full((1, 2 * B_TQ), MASK_VALUE, F32)
        l = jnp.zeros((1, 2 * B_TQ), F32)
        acc = jnp.zeros((2 * HEAD_DIM, 2 * B_TQ), F32)
        for kb in range(n_kb):
            nxt = bufs[(kb + 1) % 2]
            cmax_next = scores(qq, kb + 1, nxt) if kb + 1 < n_kb else scores(qq_next, 0, nxt)
            m, l, acc = softmax_pv(kb, bufs[kb % 2], cmax, m, l, acc)
            cmax = cmax_next
        o = acc * (1.0 / l)
        o = o[:, :B_TQ] - lam * o[:, B_TQ:]
        o = o * lax.rsqrt(jnp.mean(o * o, axis=0, keepdims=True) + SUBLN_EPS)
        q0 = pl.multiple_of(t * B_TQ, B_TQ)
        o_ref[0, pl.ds(q0, B_TQ), :] = ((o.T * g) * (1.0 - LAMBDA_INIT)).astype(BF16)
        return cmax

    assert n_kb % 2 == 0
    lax.fori_loop(0, n_qt, q_tile, scores(load_qq(0), 0, sa_ref))


def _diff_attention(qkb, vbt, lam_params, subln_g):
    b, s, _ = qkb.shape
    return pl.pallas_call(
        functools.partial(_diff_kernel, seq=s),
        grid=(b, B_HEADS),
        in_specs=[
            pl.BlockSpec((1, s, LANES), lambda i, h: (i, 0, h)),
            pl.BlockSpec((1, s, LANES), lambda i, h: (i, 0, B_HEADS + h)),
            pl.BlockSpec((1, s // B_TK, LANES, B_TK), lambda i, h: (i, 0, h, 0)),
            pl.BlockSpec((4, HEAD_DIM), lambda i, h: (0, 0)),
            pl.BlockSpec((1, 2 * HEAD_DIM), lambda i, h: (0, 0)),
        ],
        out_specs=pl.BlockSpec((1, s, LANES), lambda i, h: (i, 0, h)),
        out_shape=jax.ShapeDtypeStruct((b, s, B_WIDTH), BF16),
        scratch_shapes=[pltpu.VMEM((B_KB, 2 * B_TQ), F32), pltpu.VMEM((B_KB, 2 * B_TQ), F32)],
        compiler_params=pltpu.CompilerParams(
            dimension_semantics=("arbitrary", "arbitrary"), vmem_limit_bytes=VMEM_LIMIT_BYTES),
        name="diff_attn",
    )(qkb, qkb, vbt, lam_params, subln_g)


MERGE_TN = 512


def _split3_bf16(v):
    hi = v.astype(BF16)
    r1 = v - hi.astype(F32)
    mid = r1.astype(BF16)
    lo = (r1 - mid.astype(F32)).astype(BF16)
    return hi, mid, lo


def _merge_kernel(x_ref, g_ref, o1_ref, l1_ref, o4_ref, l4_ref, o16_ref, l16_ref, ob_ref,
                  p4_ref, p16_ref, wg_ref, wa_ref, wb_ref, wo_ref, out_ref):
    x = x_ref[0]
    tm, d = x.shape
    hb = _rms_norm(x, g_ref[...], NORM_EPS).astype(BF16)

    def token_order(o_ref_, l_ref_, p_ref_, dil):
        n = PERM // dil
        os_, ls_ = [], []
        for u in range(tm // PERM):
            xo = jnp.concatenate([o_ref_[0, r, u * n:(u + 1) * n, :] for r in range(dil)], axis=0)
            os_.append(jnp.dot(p_ref_[...], xo, preferred_element_type=F32))
            xl = jnp.concatenate([l_ref_[0, r, u * n:(u + 1) * n, :] for r in range(dil)], axis=0)
            ls_.append(sum(jnp.dot(p_ref_[...], part, preferred_element_type=F32)
                           for part in _split3_bf16(xl)))
        return jnp.concatenate(os_, axis=0), jnp.concatenate(ls_, axis=0)

    o4, l4 = token_order(o4_ref, l4_ref, p4_ref, 4)
    o16, l16 = token_order(o16_ref, l16_ref, p16_ref, 16)
    outs = [o1_ref[0, 0].astype(F32), o4, o16]
    lses = [l1_ref[0, 0], l4, l16]

    m = jnp.maximum(jnp.maximum(lses[0], lses[1]), lses[2])
    es = [jnp.exp(v - m) for v in lses]
    inv = 1.0 / (es[0] + es[1] + es[2])
    ws = [e * inv for e in es]
    lane = lax.broadcasted_iota(jnp.int32, (tm, LANES), 1)
    groups = []
    for j in range(A_WIDTH // LANES):
        acc = None
        for w, o in zip(ws, outs):
            wj = jnp.where(lane < HEAD_DIM, w[:, 2 * j:2 * j + 1], w[:, 2 * j + 1:2 * j + 2])
            term = wj * o[:, j * LANES:(j + 1) * LANES]
            acc = term if acc is None else acc + term
        groups.append(acc)
    attn_a = jnp.concatenate(groups, axis=-1).astype(BF16)
    attn_b = ob_ref[0]

    out = x
    for c in range(d // MERGE_TN):
        cs = slice(c * MERGE_TN, (c + 1) * MERGE_TN)
        ya = jnp.dot(attn_a, wa_ref[:, cs], preferred_element_type=F32)
        yb = jnp.dot(attn_b, wb_ref[:, cs], preferred_element_type=F32)
        ga = jnp.dot(hb, wg_ref[:, cs], preferred_element_type=F32)
        gb = jnp.dot(hb, wg_ref[:, d + c * MERGE_TN:d + (c + 1) * MERGE_TN], preferred_element_type=F32)
        merged = jax.nn.sigmoid(ga) * ya + jax.nn.sigmoid(gb) * yb
        out = out + jnp.dot(merged.astype(BF16), wo_ref[cs, :], preferred_element_type=F32)
    out_ref[0] = out


def _merge(x, g, o1, l1, o4, l4, o16, l16, ob, p4, p16, wg, wa, wb, wo):
    b, s, d = x.shape
    tm = TOKEN_TILE
    tok = lambda w: pl.BlockSpec((1, tm, w), lambda i, j: (i, j, 0))
    phase = lambda dil, w: pl.BlockSpec((1, dil, tm // dil, w), lambda i, j: (i, 0, j, 0))
    const = lambda a: pl.BlockSpec(a.shape, lambda i, j: (0,) * a.ndim)
    return pl.pallas_call(
        _merge_kernel,
        grid=(b, s // tm),
        in_specs=[tok(d), const(g), phase(1, A_WIDTH), phase(1, LANES), phase(4, A_WIDTH), phase(4, LANES),
                  phase(16, A_WIDTH), phase(16, LANES), tok(B_WIDTH), const(p4), const(p16),
                  const(wg), const(wa), const(wb), const(wo)],
        out_specs=tok(d),
        out_shape=jax.ShapeDtypeStruct((b, s, d), F32),
        compiler_params=pltpu.CompilerParams(
            dimension_semantics=("arbitrary", "arbitrary"), vmem_limit_bytes=VMEM_LIMIT_BYTES),
        name="merge_proj",
    )(x, g, o1, l1, o4, l4, o16, l16, ob, p4, p16, wg, wa, wb, wo)


MLP_TF = 512


def _mlp_kernel(x_ref, g_ref, w1_ref, w2_ref, gf_ref, out_ref):
    x = x_ref[...]
    hb = _rms_norm(x, g_ref[...], NORM_EPS).astype(BF16)
    acc = x
    for c in range(w1_ref.shape[1] // MLP_TF):
        cs = slice(c * MLP_TF, (c + 1) * MLP_TF)
        u = jnp.maximum(jnp.dot(hb, w1_ref[:, cs], preferred_element_type=F32), 0.0)
        acc = acc + jnp.dot((u * u).astype(BF16), w2_ref[cs, :], preferred_element_type=F32)
    out_ref[...] = _rms_norm(acc, gf_ref[...], NORM_EPS)


def _mlp(x2d, g, w1, w2, gf):
    t, d = x2d.shape
    tm = TOKEN_TILE
    row = pl.BlockSpec((tm, d), lambda i: (i, 0))
    full = lambda a: pl.BlockSpec(a.shape, lambda i: (0, 0))
    return pl.pallas_call(
        _mlp_kernel,
        grid=(t // tm,),
        in_specs=[row, full(g), full(w1), full(w2), full(gf)],
        out_specs=row,
        out_shape=jax.ShapeDtypeStruct((t, d), F32),
        compiler_params=pltpu.CompilerParams(
            dimension_semantics=("arbitrary",), vmem_limit_bytes=VMEM_LIMIT_BYTES),
        name="mlp_norm",
    )(x2d, g, w1, w2, gf)


def _rope_tables(seq):
    pos = jnp.arange(seq, dtype=F32)
    inv_freq = ROPE_THETA ** (-jnp.arange(0, HEAD_DIM, 2, dtype=F32) / HEAD_DIM)
    ang = pos[:, None] * inv_freq[None, :]
    cos = jnp.cos(ang)
    sin = jnp.sin(ang)
    cos = jnp.concatenate([cos, cos, cos, cos], axis=-1)
    sin = jnp.concatenate([-sin, sin, -sin, sin], axis=-1)
    return cos, sin


def kernel(x, w_in, w_branch_a, w_branch_b, w_out, lambda_q1, lambda_k1, lambda_q2, lambda_k2,
           diff_subln_g, norm_mix_g, norm_mlp_g, w_ff1, w_ff2, norm_final_g):
    b, s, d = x.shape
    assert w_in.shape[0] == 1, "single-layer block: the final norm is fused into the MLP kernel"
    assert s % B_KB == 0 and s % TOKEN_TILE == 0
    cos, sin = _rope_tables(s)
    w_in_l = w_in[0].astype(BF16)
    g_mix = norm_mix_g[0].reshape(1, d)
    vb_lo = QKV_A_WIDTH + QK_B_WIDTH
    p4 = jnp.asarray(_phase_perm(4), BF16)
    p16 = jnp.asarray(_phase_perm(16), BF16)
    qkva, qkb, vbt, ph4, ph16 = _qkv_proj(
        x, g_mix, w_in_l[:, :vb_lo], w_in_l[:, vb_lo:QKV_WIDTH].T, cos, sin, p4, p16)
    o1, l1 = _band_attention(qkva.reshape(b, 1, s, QKV_A_WIDTH), DILATED_PATTERNS[0][0])
    o4, l4 = _band_attention(ph4, DILATED_PATTERNS[1][0])
    o16, l16 = _band_attention(ph16, DILATED_PATTERNS[2][0])
    lam_params = jnp.stack([lambda_q1[0], lambda_k1[0], lambda_q2[0], lambda_k2[0]]).astype(F32)
    ob = _diff_attention(qkb, vbt, lam_params, diff_subln_g[0].reshape(1, 2 * HEAD_DIM).astype(F32))
    x1 = _merge(x, g_mix, o1, l1, o4, l4, o16, l16, ob, p4.T, p16.T, w_in_l[:, QKV_WIDTH:],
                w_branch_a[0].astype(BF16), w_branch_b[0].astype(BF16), w_out[0].astype(BF16))
    out = _mlp(x1.reshape(b * s, d), norm_mlp_g[0].reshape(1, d), w_ff1[0].astype(BF16),
               w_ff2[0].astype(BF16), norm_final_g.reshape(1, d))
    return out.reshape(b, s, d)
```

```python
import functools
import math

import numpy as np
import jax
import jax.numpy as jnp
from jax import lax
from jax.experimental import pallas as pl
from jax.experimental.pallas import tpu as pltpu

HEAD_DIM = 64
HALF_DIM = HEAD_DIM // 2
LANES = 128
A_HEADS = 8
A_WIDTH = A_HEADS * HEAD_DIM
B_HEADS = 4
B_WIDTH = B_HEADS * 2 * HEAD_DIM
QKV_A_WIDTH = 3 * A_WIDTH
QK_B_WIDTH = 2 * B_WIDTH
QKV_WIDTH = QKV_A_WIDTH + QK_B_WIDTH + B_WIDTH
DILATED_PATTERNS = ((128, 1), (512, 4), (2048, 16))
ROPE_THETA = 10000.0
NORM_EPS = 1e-6
SUBLN_EPS = 1e-5
LAMBDA_INIT = 0.8 - 0.6 * math.exp(-0.3 * 0)
QK_SCALE_LOG2E = math.log2(math.e) / math.sqrt(HEAD_DIM)
MASK_VALUE = -1e30
VMEM_LIMIT_BYTES = 56 * 1024 * 1024
PERM = 256
TOKEN_TILE = 512

BF16 = jnp.bfloat16
F32 = jnp.float32
NT_DIMS = (((1,), (1,)), ((), ()))


def _rms_norm(x, g, eps):
    return (x * lax.rsqrt(jnp.mean(x * x, axis=-1, keepdims=True) + eps)) * g


def _split_heads_rows(q):
    lane = lax.broadcasted_iota(jnp.int32, q.shape, 1)
    zero = jnp.zeros_like(q)
    return jnp.concatenate(
        [jnp.where(lane < HEAD_DIM, q, zero), jnp.where(lane >= HEAD_DIM, q, zero)], axis=0)


def _phase_perm(dil):
    n = PERM // dil
    p = np.zeros((PERM, PERM), np.float32)
    for r in range(dil):
        for i in range(n):
            p[r * n + i, i * dil + r] = 1.0
    return p


def _qkv_kernel(x_ref, g_ref, w_ref, wvt_ref, cos_ref, sin_ref, p4_ref, p16_ref,
                qkva_ref, qkb_ref, vbt_ref, ph4_ref, ph16_ref):
    x = x_ref[0]
    ts = x.shape[0]
    hb = _rms_norm(x, g_ref[...], NORM_EPS).astype(BF16)
    cos = cos_ref[...]
    sin = sin_ref[...]
    lane = lax.broadcasted_iota(jnp.int32, cos.shape, 1)
    first_half = (lane % HEAD_DIM) < HALF_DIM

    def rope(t, scale):
        parts = []
        for j in range(t.shape[1] // LANES):
            tj = t[:, j * LANES:(j + 1) * LANES]
            rot = jnp.where(first_half, pltpu.roll(tj, LANES - HALF_DIM, 1), pltpu.roll(tj, HALF_DIM, 1))
            tj = tj * cos + rot * sin
            parts.append((tj * scale if scale != 1.0 else tj).astype(BF16))
        return parts

    for c in range(5):
        t = jnp.dot(hb, w_ref[:, c * A_WIDTH:(c + 1) * A_WIDTH], preferred_element_type=F32)
        dst, base = (qkva_ref, c * A_WIDTH) if c < 3 else (qkb_ref, (c - 3) * A_WIDTH)
        if c == 2:
            dst[0, :, base:base + A_WIDTH] = t.astype(BF16)
        else:
            scale = QK_SCALE_LOG2E if c in (0, 3) else 1.0
            for j, tj in enumerate(rope(t, scale)):
                dst[0, :, base + j * LANES:base + (j + 1) * LANES] = tj
    vbt = lax.dot_general(wvt_ref[...], hb, NT_DIMS, preferred_element_type=F32).astype(BF16)
    for u in range(ts // B_TK):
        vbt_ref[0, u] = vbt[:, u * B_TK:(u + 1) * B_TK]

    for dil, p_ref, ph_ref in ((4, p4_ref, ph4_ref), (16, p16_ref, ph16_ref)):
        n = PERM // dil
        for u in range(ts // PERM):
            for c in range(3):
                cs = slice(c * A_WIDTH, (c + 1) * A_WIDTH)
                tb = qkva_ref[0, u * PERM:(u + 1) * PERM, cs]
                ph = jnp.dot(p_ref[...], tb, preferred_element_type=F32).astype(BF16)
                for r in range(dil):
                    ph_ref[0, r, u * n:(u + 1) * n, cs] = ph[r * n:(r + 1) * n]


def _qkv_proj(x, g, w_qkv, w_vbt, cos, sin, p4, p16):
    b, s, d = x.shape
    ts = TOKEN_TILE
    const = lambda a: pl.BlockSpec(a.shape, lambda i, j: (0,) * a.ndim)
    return pl.pallas_call(
        _qkv_kernel,
        grid=(b, s // ts),
        in_specs=[
            pl.BlockSpec((1, ts, d), lambda i, j: (i, j, 0)),
            const(g), const(w_qkv), const(w_vbt),
            pl.BlockSpec((ts, LANES), lambda i, j: (j, 0)),
            pl.BlockSpec((ts, LANES), lambda i, j: (j, 0)),
            const(p4), const(p16),
        ],
        out_specs=[
            pl.BlockSpec((1, ts, QKV_A_WIDTH), lambda i, j: (i, j, 0)),
            pl.BlockSpec((1, ts, QK_B_WIDTH), lambda i, j: (i, j, 0)),
            pl.BlockSpec((1, ts // B_TK, B_WIDTH, B_TK), lambda i, j: (i, j, 0, 0)),
            pl.BlockSpec((1, 4, ts // 4, QKV_A_WIDTH), lambda i, j: (i, 0, j, 0)),
            pl.BlockSpec((1, 16, ts // 16, QKV_A_WIDTH), lambda i, j: (i, 0, j, 0)),
        ],
        out_shape=[
            jax.ShapeDtypeStruct((b, s, QKV_A_WIDTH), BF16),
            jax.ShapeDtypeStruct((b, s, QK_B_WIDTH), BF16),
            jax.ShapeDtypeStruct((b, s // B_TK, B_WIDTH, B_TK), BF16),
            jax.ShapeDtypeStruct((b, 4, s // 4, QKV_A_WIDTH), BF16),
            jax.ShapeDtypeStruct((b, 16, s // 16, QKV_A_WIDTH), BF16),
        ],
        compiler_params=pltpu.CompilerParams(
            dimension_semantics=("arbitrary", "arbitrary"), vmem_limit_bytes=VMEM_LIMIT_BYTES),
        name="qkv_proj",
    )(x, g, w_qkv, w_vbt, cos, sin, p4, p16)


A_TQ = 128
A_WIN = 256


def _band_bias():
    row = np.arange(2 * A_TQ)[:, None] % A_TQ
    col = np.arange(A_WIN)[None, :]
    half = (A_WIN - A_TQ) // 2
    return np.stack([np.where(np.abs(col - row + d) <= half, 0.0, MASK_VALUE)
                     for d in (-half, 0, -2 * half)]).astype(np.float32)


def _band_kernel(q_ref, k_ref, v_ref, bias_ref, o_ref, lse_ref, *, half, ls, rows_per_step):
    step = pl.program_id(2)
    n_sub = rows_per_step // A_TQ
    out_lane = lax.broadcasted_iota(jnp.int32, (A_TQ, LANES), 1)

    def sub_tile(i, carry):
        r0 = pl.multiple_of(i * A_TQ, A_TQ)
        t0 = step * rows_per_step + r0
        start = pl.multiple_of(jnp.clip(t0 - half, 0, ls - A_WIN), half)
        which = jnp.where(t0 == 0, 1, jnp.where(t0 == ls - A_TQ, 2, 0))
        for j in range(A_WIDTH // LANES):
            cs = slice(j * LANES, (j + 1) * LANES)
            qq = _split_heads_rows(q_ref[0, 0, pl.ds(r0, A_TQ), cs])
            kw = k_ref[0, 0, pl.ds(start, A_WIN), cs]
            vw = v_ref[0, 0, pl.ds(start, A_WIN), cs]
            s = lax.dot_general(qq, kw, NT_DIMS, preferred_element_type=F32) + bias_ref[which]
            m = jnp.max(s, axis=-1, keepdims=True)
            p = jnp.exp2(s - m)
            l = jnp.sum(p, axis=-1, keepdims=True)
            pv = jnp.dot(p.astype(BF16), vw, preferred_element_type=F32) * (1.0 / l)
            o_ref[0, 0, pl.ds(r0, A_TQ), cs] = jnp.where(
                out_lane < HEAD_DIM, pv[:A_TQ], pv[A_TQ:]).astype(BF16)
            lse = m + jnp.log2(l)
            lse_ref[0, 0, pl.ds(r0, A_TQ), cs] = jnp.where(out_lane < HEAD_DIM, lse[:A_TQ], lse[A_TQ:])
        return carry

    lax.fori_loop(0, n_sub, sub_tile, 0, unroll=4)


def _band_attention(ph, window):
    b, dil, ls, _ = ph.shape
    half = window // (2 * dil)
    assert half == A_TQ // 2 and A_WIN == A_TQ + 2 * half and ls % A_TQ == 0 and ls >= A_WIN
    rows_per_step = min(ls, 512)
    kern = functools.partial(_band_kernel, half=half, ls=ls, rows_per_step=rows_per_step)
    return pl.pallas_call(
        kern,
        grid=(b, dil, ls // rows_per_step),
        in_specs=[
            pl.BlockSpec((1, 1, rows_per_step, A_WIDTH), lambda i, r, t: (i, r, t, 0)),
            pl.BlockSpec((1, 1, ls, A_WIDTH), lambda i, r, t: (i, r, 0, 1)),
            pl.BlockSpec((1, 1, ls, A_WIDTH), lambda i, r, t: (i, r, 0, 2)),
            pl.BlockSpec((3, 2 * A_TQ, A_WIN), lambda i, r, t: (0, 0, 0)),
        ],
        out_specs=[
            pl.BlockSpec((1, 1, rows_per_step, A_WIDTH), lambda i, r, t: (i, r, t, 0)),
            pl.BlockSpec((1, 1, rows_per_step, A_WIDTH), lambda i, r, t: (i, r, t, 0)),
        ],
        out_shape=[
            jax.ShapeDtypeStruct((b, dil, ls, A_WIDTH), BF16),
            jax.ShapeDtypeStruct((b, dil, ls, A_WIDTH), F32),
        ],
        compiler_params=pltpu.CompilerParams(
            dimension_semantics=("arbitrary", "arbitrary", "arbitrary"),
            vmem_limit_bytes=VMEM_LIMIT_BYTES),
        name=f"band_attn_d{dil}",
    )(ph, ph, ph, jnp.asarray(_band_bias()))


B_TQ = 256
B_KB = 1024
B_TK = 256


def _diff_kernel(q_ref, k_ref, vt_ref, lam_ref, g_ref, o_ref, sa_ref, sb_ref, *, seq):
    lam_p = lam_ref[...]
    lam = (jnp.exp(jnp.sum(lam_p[0:1] * lam_p[1:2], axis=-1, keepdims=True))
           - jnp.exp(jnp.sum(lam_p[2:3] * lam_p[3:4], axis=-1, keepdims=True)) + LAMBDA_INIT)
    g = g_ref[...]

    n_kb = seq // B_KB
    n_qt = seq // B_TQ
    bufs = (sa_ref, sb_ref)

    def load_qq(t):
        return _split_heads_rows(q_ref[0, pl.ds(pl.multiple_of(t * B_TQ, B_TQ), B_TQ), :])

    def scores(qq, kb, s_ref):
        s = lax.dot_general(k_ref[0, kb * B_KB:(kb + 1) * B_KB, :], qq, NT_DIMS, preferred_element_type=F32)
        s_ref[...] = s
        return jnp.max(s, axis=0, keepdims=True)

    def softmax_pv(kb, s_ref, cmax, m, l, acc):
        m_new = jnp.maximum(m, cmax)
        alpha = jnp.exp2(m - m_new)
        l = alpha * l
        acc = alpha * acc
        for c in range(B_KB // B_TK):
            p = jnp.exp2(s_ref[c * B_TK:(c + 1) * B_TK, :] - m_new)
            l = l + jnp.sum(p, axis=0, keepdims=True)
            acc = acc + jnp.dot(vt_ref[0, kb * (B_KB // B_TK) + c], p.astype(BF16),
                                preferred_element_type=F32)
        return m_new, l, acc

    def finish(t, l, acc):
        o = acc * (1.0 / l)
        o = o[:, :B_TQ] - lam * o[:, B_TQ:]
        o = o * lax.rsqrt(jnp.mean(o * o, axis=0, keepdims=True) + SUBLN_EPS)
        q0 = pl.multiple_of(t * B_TQ, B_TQ)
        o_ref[0, pl.ds(q0, B_TQ), :] = ((o.T * g) * (1.0 - LAMBDA_INIT)).astype(BF16)

    def q_tile(t, carry):
        cmax, l_prev, acc_prev = carry
        finish(jnp.maximum(t - 1, 0), l_prev, acc_prev)
        qq = load_qq(t)
        qq_next = load_qq(jnp.minimum(t + 1, n_qt - 1))
        m = jnp.full((1, 2 * B_TQ), MASK_VALUE, F32)
        l = jnp.zeros((1, 2 * B_TQ), F32)
        acc = jnp.zeros((2 * HEAD_DIM, 2 * B_TQ), F32)
        for kb in range(n_kb):
            nxt = bufs[(kb + 1) % 2]
            cmax_next = scores(qq, kb + 1, nxt) if kb + 1 < n_kb else scores(qq_next, 0, nxt)
            m, l, acc = softmax_pv(kb, bufs[kb % 2], cmax, m, l, acc)
            cmax = cmax_next
        return cmax, l, acc

    assert n_kb % 2 == 0 and n_qt >= 2
    init = (scores(load_qq(0), 0, sa_ref), jnp.ones((1, 2 * B_TQ), F32),
            jnp.zeros((2 * HEAD_DIM, 2 * B_TQ), F32))
    _, l, acc = lax.fori_loop(0, n_qt, q_tile, init)
    finish(n_qt - 1, l, acc)


def _diff_attention(qkb, vbt, lam_params, subln_g):
    b, s, _ = qkb.shape
    return pl.pallas_call(
        functools.partial(_diff_kernel, seq=s),
        grid=(b, B_HEADS),
        in_specs=[
            pl.BlockSpec((1, s, LANES), lambda i, h: (i, 0, h)),
            pl.BlockSpec((1, s, LANES), lambda i, h: (i, 0, B_HEADS + h)),
            pl.BlockSpec((1, s // B_TK, LANES, B_TK), lambda i, h: (i, 0, h, 0)),
            pl.BlockSpec((4, HEAD_DIM), lambda i, h: (0, 0)),
            pl.BlockSpec((1, 2 * HEAD_DIM), lambda i, h: (0, 0)),
        ],
        out_specs=pl.BlockSpec((1, s, LANES), lambda i, h: (i, 0, h)),
        out_shape=jax.ShapeDtypeStruct((b, s, B_WIDTH), BF16),
        scratch_shapes=[pltpu.VMEM((B_KB, 2 * B_TQ), F32), pltpu.VMEM((B_KB, 2 * B_TQ), F32)],
        compiler_params=pltpu.CompilerParams(
            dimension_semantics=("arbitrary", "arbitrary"), vmem_limit_bytes=VMEM_LIMIT_BYTES),
        name="diff_attn",
    )(qkb, qkb, vbt, lam_params, subln_g)


MERGE_TN = 512


def _split3_bf16(v):
    hi = v.astype(BF16)
    r1 = v - hi.astype(F32)
    mid = r1.astype(BF16)
    lo = (r1 - mid.astype(F32)).astype(BF16)
    return hi, mid, lo


def _merge_kernel(x_ref, g_ref, o1_ref, l1_ref, o4_ref, l4_ref, o16_ref, l16_ref, ob_ref,
                  p4_ref, p16_ref, wg_ref, wa_ref, wb_ref, wo_ref, out_ref):
    x = x_ref[0]
    tm, d = x.shape
    hb = _rms_norm(x, g_ref[...], NORM_EPS).astype(BF16)

    def token_order(o_ref_, l_ref_, p_ref_, dil):
        n = PERM // dil
        os_, ls_ = [], []
        for u in range(tm // PERM):
            xo = jnp.concatenate([o_ref_[0, r, u * n:(u + 1) * n, :] for r in range(dil)], axis=0)
            os_.append(jnp.dot(p_ref_[...], xo, preferred_element_type=F32))
            xl = jnp.concatenate([l_ref_[0, r, u * n:(u + 1) * n, :] for r in range(dil)], axis=0)
            ls_.append(sum(jnp.dot(p_ref_[...], part, preferred_element_type=F32)
                           for part in _split3_bf16(xl)))
        return jnp.concatenate(os_, axis=0), jnp.concatenate(ls_, axis=0)

    o4, l4 = token_order(o4_ref, l4_ref, p4_ref, 4)
    o16, l16 = token_order(o16_ref, l16_ref, p16_ref, 16)
    outs = [o1_ref[0, 0].astype(F32), o4, o16]
    lses = [l1_ref[0, 0], l4, l16]

    m = jnp.maximum(jnp.maximum(lses[0], lses[1]), lses[2])
    es = [jnp.exp2(v - m) for v in lses]
    num = es[0] * outs[0] + es[1] * outs[1] + es[2] * outs[2]
    attn_a = (num * (1.0 / (es[0] + es[1] + es[2]))).astype(BF16)
    attn_b = ob_ref[0]

    out = x
    for c in range(d // MERGE_TN):
        cs = slice(c * MERGE_TN, (c + 1) * MERGE_TN)
        ya = jnp.dot(attn_a, wa_ref[:, cs], preferred_element_type=F32)
        yb = jnp.dot(attn_b, wb_ref[:, cs], preferred_element_type=F32)
        ga = jnp.dot(hb, wg_ref[:, cs], preferred_element_type=F32)
        gb = jnp.dot(hb, wg_ref[:, d + c * MERGE_TN:d + (c + 1) * MERGE_TN], preferred_element_type=F32)
        merged = jax.nn.sigmoid(ga) * ya + jax.nn.sigmoid(gb) * yb
        out = out + jnp.dot(merged.astype(BF16), wo_ref[cs, :], preferred_element_type=F32)
    out_ref[0] = out


def _merge(x, g, o1, l1, o4, l4, o16, l16, ob, p4, p16, wg, wa, wb, wo):
    b, s, d = x.shape
    tm = TOKEN_TILE
    tok = lambda w: pl.BlockSpec((1, tm, w), lambda i, j: (i, j, 0))
    phase = lambda dil, w: pl.BlockSpec((1, dil, tm // dil, w), lambda i, j: (i, 0, j, 0))
    const = lambda a: pl.BlockSpec(a.shape, lambda i, j: (0,) * a.ndim)
    return pl.pallas_call(
        _merge_kernel,
        grid=(b, s // tm),
        in_specs=[tok(d), const(g), phase(1, A_WIDTH), phase(1, A_WIDTH), phase(4, A_WIDTH), phase(4, A_WIDTH),
                  phase(16, A_WIDTH), phase(16, A_WIDTH), tok(B_WIDTH), const(p4), const(p16),
                  const(wg), const(wa), const(wb), const(wo)],
        out_specs=tok(d),
        out_shape=jax.ShapeDtypeStruct((b, s, d), F32),
        compiler_params=pltpu.CompilerParams(
            dimension_semantics=("arbitrary", "arbitrary"), vmem_limit_bytes=VMEM_LIMIT_BYTES),
        name="merge_proj",
    )(x, g, o1, l1, o4, l4, o16, l16, ob, p4, p16, wg, wa, wb, wo)


MLP_TF = 512


def _mlp_kernel(x_ref, g_ref, w1_ref, w2_ref, gf_ref, out_ref):
    x = x_ref[...]
    hb = _rms_norm(x, g_ref[...], NORM_EPS).astype(BF16)
    acc = x
    for c in range(w1_ref.shape[1] // MLP_TF):
        cs = slice(c * MLP_TF, (c + 1) * MLP_TF)
        u = jnp.maximum(jnp.dot(hb, w1_ref[:, cs], preferred_element_type=F32), 0.0)
        acc = acc + jnp.dot((u * u).astype(BF16), w2_ref[cs, :], preferred_element_type=F32)
    out_ref[...] = _rms_norm(acc, gf_ref[...], NORM_EPS)


def _mlp(x2d, g, w1, w2, gf):
    t, d = x2d.shape
    tm = TOKEN_TILE
    row = pl.BlockSpec((tm, d), lambda i: (i, 0))
    full = lambda a: pl.BlockSpec(a.shape, lambda i: (0, 0))
    return pl.pallas_call(
        _mlp_kernel,
        grid=(t // tm,),
        in_specs=[row, full(g), full(w1), full(w2), full(gf)],
        out_specs=row,
        out_shape=jax.ShapeDtypeStruct((t, d), F32),
        compiler_params=pltpu.CompilerParams(
            dimension_semantics=("arbitrary",), vmem_limit_bytes=VMEM_LIMIT_BYTES),
        name="mlp_norm",
    )(x2d, g, w1, w2, gf)


def _rope_tables(seq):
    pos = jnp.arange(seq, dtype=F32)
    inv_freq = ROPE_THETA ** (-jnp.arange(0, HEAD_DIM, 2, dtype=F32) / HEAD_DIM)
    ang = pos[:, None] * inv_freq[None, :]
    cos = jnp.cos(ang)
    sin = jnp.sin(ang)
    cos = jnp.concatenate([cos, cos, cos, cos], axis=-1)
    sin = jnp.concatenate([-sin, sin, -sin, sin], axis=-1)
    return cos, sin


def kernel(x, w_in, w_branch_a, w_branch_b, w_out, lambda_q1, lambda_k1, lambda_q2, lambda_k2,
           diff_subln_g, norm_mix_g, norm_mlp_g, w_ff1, w_ff2, norm_final_g):
    b, s, d = x.shape
    assert w_in.shape[0] == 1, "single-layer block: the final norm is fused into the MLP kernel"
    assert s % B_KB == 0 and s % TOKEN_TILE == 0
    cos, sin = _rope_tables(s)
    w_in_l = w_in[0].astype(BF16)
    g_mix = norm_mix_g[0].reshape(1, d)
    vb_lo = QKV_A_WIDTH + QK_B_WIDTH
    p4 = jnp.asarray(_phase_perm(4), BF16)
    p16 = jnp.asarray(_phase_perm(16), BF16)
    qkva, qkb, vbt, ph4, ph16 = _qkv_proj(
        x, g_mix, w_in_l[:, :vb_lo], w_in_l[:, vb_lo:QKV_WIDTH].T, cos, sin, p4, p16)
    o1, l1 = _band_attention(qkva.reshape(b, 1, s, QKV_A_WIDTH), DILATED_PATTERNS[0][0])
    o4, l4 = _band_attention(ph4, DILATED_PATTERNS[1][0])
    o16, l16 = _band_attention(ph16, DILATED_PATTERNS[2][0])
    lam_params = jnp.stack([lambda_q1[0], lambda_k1[0], lambda_q2[0], lambda_k2[0]]).astype(F32)
    ob = _diff_attention(qkb, vbt, lam_params, diff_subln_g[0].reshape(1, 2 * HEAD_DIM).astype(F32))
    x1 = _merge(x, g_mix, o1, l1, o4, l4, o16, l16, ob, p4.T, p16.T, w_in_l[:, QKV_WIDTH:],
                w_branch_a[0].astype(BF16), w_branch_b[0].astype(BF16), w_out[0].astype(BF16))
    out = _mlp(x1.reshape(b * s, d), norm_mlp_g[0].reshape(1, d), w_ff1[0].astype(BF16),
               w_ff2[0].astype(BF16), norm_final_g.reshape(1, d))
    return out.reshape(b, s, d)
```

```python
import functools
import math

import numpy as np
import jax
import jax.numpy as jnp
from jax import lax
from jax.experimental import pallas as pl
from jax.experimental.pallas import tpu as pltpu

HEAD_DIM = 64
HALF_DIM = HEAD_DIM // 2
LANES = 128
A_HEADS = 8
A_WIDTH = A_HEADS * HEAD_DIM
B_HEADS = 4
B_WIDTH = B_HEADS * 2 * HEAD_DIM
QKV_A_WIDTH = 3 * A_WIDTH
QK_B_WIDTH = 2 * B_WIDTH
QKV_WIDTH = QKV_A_WIDTH + QK_B_WIDTH + B_WIDTH
DILATED_PATTERNS = ((128, 1), (512, 4), (2048, 16))
ROPE_THETA = 10000.0
NORM_EPS = 1e-6
SUBLN_EPS = 1e-5
LAMBDA_INIT = 0.8 - 0.6 * math.exp(-0.3 * 0)
QK_SCALE_LOG2E = math.log2(math.e) / math.sqrt(HEAD_DIM)
MASK_VALUE = -1e30
VMEM_LIMIT_BYTES = 56 * 1024 * 1024
PERM = 256
TOKEN_TILE = 512

BF16 = jnp.bfloat16
F32 = jnp.float32
NT_DIMS = (((1,), (1,)), ((), ()))


def _rms_norm(x, g, eps):
    return (x * lax.rsqrt(jnp.mean(x * x, axis=-1, keepdims=True) + eps)) * g


def _split_heads_rows(q):
    lane = lax.broadcasted_iota(jnp.int32, q.shape, 1)
    zero = jnp.zeros_like(q)
    return jnp.concatenate(
        [jnp.where(lane < HEAD_DIM, q, zero), jnp.where(lane >= HEAD_DIM, q, zero)], axis=0)


def _phase_perm(dil):
    n = PERM // dil
    p = np.zeros((PERM, PERM), np.float32)
    for r in range(dil):
        for i in range(n):
            p[r * n + i, i * dil + r] = 1.0
    return p


def _qkv_kernel(x_ref, g_ref, w_ref, wvt_ref, cos_ref, sin_ref, p4_ref, p16_ref,
                qkva_ref, qkb_ref, vbt_ref, ph4_ref, ph16_ref):
    x = x_ref[0]
    ts = x.shape[0]
    hb = _rms_norm(x, g_ref[...], NORM_EPS).astype(BF16)
    cos = cos_ref[...]
    sin = sin_ref[...]
    lane = lax.broadcasted_iota(jnp.int32, cos.shape, 1)
    first_half = (lane % HEAD_DIM) < HALF_DIM

    def rope(t, scale):
        parts = []
        for j in range(t.shape[1] // LANES):
            tj = t[:, j * LANES:(j + 1) * LANES]
            rot = jnp.where(first_half, pltpu.roll(tj, LANES - HALF_DIM, 1), pltpu.roll(tj, HALF_DIM, 1))
            tj = tj * cos + rot * sin
            parts.append((tj * scale if scale != 1.0 else tj).astype(BF16))
        return parts

    for c in range(5):
        t = jnp.dot(hb, w_ref[:, c * A_WIDTH:(c + 1) * A_WIDTH], preferred_element_type=F32)
        dst, base = (qkva_ref, c * A_WIDTH) if c < 3 else (qkb_ref, (c - 3) * A_WIDTH)
        if c == 2:
            dst[0, :, base:base + A_WIDTH] = t.astype(BF16)
        else:
            scale = QK_SCALE_LOG2E if c in (0, 3) else 1.0
            for j, tj in enumerate(rope(t, scale)):
                dst[0, :, base + j * LANES:base + (j + 1) * LANES] = tj
    vbt = lax.dot_general(wvt_ref[...], hb, NT_DIMS, preferred_element_type=F32).astype(BF16)
    for u in range(ts // B_TK):
        vbt_ref[0, u] = vbt[:, u * B_TK:(u + 1) * B_TK]

    for dil, p_ref, ph_ref in ((4, p4_ref, ph4_ref), (16, p16_ref, ph16_ref)):
        n = PERM // dil
        for u in range(ts // PERM):
            for c in range(3):
                cs = slice(c * A_WIDTH, (c + 1) * A_WIDTH)
                tb = qkva_ref[0, u * PERM:(u + 1) * PERM, cs]
                ph = jnp.dot(p_ref[...], tb, preferred_element_type=F32).astype(BF16)
                for r in range(dil):
                    ph_ref[0, r, u * n:(u + 1) * n, cs] = ph[r * n:(r + 1) * n]


def _qkv_proj(x, g, w_qkv, w_vbt, cos, sin, p4, p16):
    b, s, d = x.shape
    ts = TOKEN_TILE
    const = lambda a: pl.BlockSpec(a.shape, lambda i, j: (0,) * a.ndim)
    return pl.pallas_call(
        _qkv_kernel,
        grid=(b, s // ts),
        in_specs=[
            pl.BlockSpec((1, ts, d), lambda i, j: (i, j, 0)),
            const(g), const(w_qkv), const(w_vbt),
            pl.BlockSpec((ts, LANES), lambda i, j: (j, 0)),
            pl.BlockSpec((ts, LANES), lambda i, j: (j, 0)),
            const(p4), const(p16),
        ],
        out_specs=[
            pl.BlockSpec((1, ts, QKV_A_WIDTH), lambda i, j: (i, j, 0)),
            pl.BlockSpec((1, ts, QK_B_WIDTH), lambda i, j: (i, j, 0)),
            pl.BlockSpec((1, ts // B_TK, B_WIDTH, B_TK), lambda i, j: (i, j, 0, 0)),
            pl.BlockSpec((1, 4, ts // 4, QKV_A_WIDTH), lambda i, j: (i, 0, j, 0)),
            pl.BlockSpec((1, 16, ts // 16, QKV_A_WIDTH), lambda i, j: (i, 0, j, 0)),
        ],
        out_shape=[
            jax.ShapeDtypeStruct((b, s, QKV_A_WIDTH), BF16),
            jax.ShapeDtypeStruct((b, s, QK_B_WIDTH), BF16),
            jax.ShapeDtypeStruct((b, s // B_TK, B_WIDTH, B_TK), BF16),
            jax.ShapeDtypeStruct((b, 4, s // 4, QKV_A_WIDTH), BF16),
            jax.ShapeDtypeStruct((b, 16, s // 16, QKV_A_WIDTH), BF16),
        ],
        compiler_params=pltpu.CompilerParams(
            dimension_semantics=("arbitrary", "arbitrary"), vmem_limit_bytes=VMEM_LIMIT_BYTES),
        name="qkv_proj",
    )(x, g, w_qkv, w_vbt, cos, sin, p4, p16)


A_TQ = 128
A_WIN = 256
BAND_ROWS = 512


def _band_bias():
    row = np.arange(2 * A_TQ)[:, None] % A_TQ
    col = np.arange(A_WIN)[None, :]
    half = (A_WIN - A_TQ) // 2
    return np.stack([np.where(np.abs(col - row + d) <= half, 0.0, MASK_VALUE)
                     for d in (-half, 0, -2 * half)]).astype(np.float32)


def _band_kernel(q_ref, k_ref, v_ref, bias_ref, o_ref, lse_ref, *, half, ls, rows_per_step):
    step = pl.program_id(2)
    n_sub = rows_per_step // A_TQ
    n_phase = q_ref.shape[1]
    out_lane = lax.broadcasted_iota(jnp.int32, (A_TQ, LANES), 1)

    def sub_tile(u, carry):
        ph = u // n_sub
        r0 = pl.multiple_of((u % n_sub) * A_TQ, A_TQ)
        t0 = step * rows_per_step + r0
        start = pl.multiple_of(jnp.clip(t0 - half, 0, ls - A_WIN), half)
        which = jnp.where(t0 == 0, 1, jnp.where(t0 == ls - A_TQ, 2, 0))
        for j in range(A_WIDTH // LANES):
            cs = slice(j * LANES, (j + 1) * LANES)
            qq = _split_heads_rows(q_ref[0, ph, pl.ds(r0, A_TQ), cs])
            kw = k_ref[0, ph, pl.ds(start, A_WIN), cs]
            vw = v_ref[0, ph, pl.ds(start, A_WIN), cs]
            s = lax.dot_general(qq, kw, NT_DIMS, preferred_element_type=F32) + bias_ref[which]
            m = jnp.max(s, axis=-1, keepdims=True)
            p = jnp.exp2(s - m)
            l = jnp.sum(p, axis=-1, keepdims=True)
            pv = jnp.dot(p.astype(BF16), vw, preferred_element_type=F32) * (1.0 / l)
            o_ref[0, ph, pl.ds(r0, A_TQ), cs] = jnp.where(
                out_lane < HEAD_DIM, pv[:A_TQ], pv[A_TQ:]).astype(BF16)
            lse = m + jnp.log2(l)
            lse_ref[0, ph, pl.ds(r0, A_TQ), cs] = jnp.where(out_lane < HEAD_DIM, lse[:A_TQ], lse[A_TQ:])
        return carry

    lax.fori_loop(0, n_phase * n_sub, sub_tile, 0, unroll=4)


def _band_attention(ph, window):
    b, dil, ls, _ = ph.shape
    half = window // (2 * dil)
    assert half == A_TQ // 2 and A_WIN == A_TQ + 2 * half and ls % A_TQ == 0 and ls >= A_WIN
    rows_per_step = min(ls, BAND_ROWS)
    n_phase = BAND_ROWS // rows_per_step
    assert dil % n_phase == 0
    kern = functools.partial(_band_kernel, half=half, ls=ls, rows_per_step=rows_per_step)
    return pl.pallas_call(
        kern,
        grid=(b, dil // n_phase, ls // rows_per_step),
        in_specs=[
            pl.BlockSpec((1, n_phase, rows_per_step, A_WIDTH), lambda i, r, t: (i, r, t, 0)),
            pl.BlockSpec((1, n_phase, ls, A_WIDTH), lambda i, r, t: (i, r, 0, 1)),
            pl.BlockSpec((1, n_phase, ls, A_WIDTH), lambda i, r, t: (i, r, 0, 2)),
            pl.BlockSpec((3, 2 * A_TQ, A_WIN), lambda i, r, t: (0, 0, 0)),
        ],
        out_specs=[
            pl.BlockSpec((1, n_phase, rows_per_step, A_WIDTH), lambda i, r, t: (i, r, t, 0)),
            pl.BlockSpec((1, n_phase, rows_per_step, A_WIDTH), lambda i, r, t: (i, r, t, 0)),
        ],
        out_shape=[
            jax.ShapeDtypeStruct((b, dil, ls, A_WIDTH), BF16),
            jax.ShapeDtypeStruct((b, dil, ls, A_WIDTH), F32),
        ],
        compiler_params=pltpu.CompilerParams(
            dimension_semantics=("arbitrary", "arbitrary", "arbitrary"),
            vmem_limit_bytes=VMEM_LIMIT_BYTES),
        name=f"band_attn_d{dil}",
    )(ph, ph, ph, jnp.asarray(_band_bias()))


B_TQ = 256
B_KB = 1024
B_TK = 256


def _diff_kernel(q_ref, k_ref, vt_ref, lam_ref, g_ref, o_ref, sa_ref, sb_ref, *, seq):
    lam_p = lam_ref[...]
    lam = (jnp.exp(jnp.sum(lam_p[0:1] * lam_p[1:2], axis=-1, keepdims=True))
           - jnp.exp(jnp.sum(lam_p[2:3] * lam_p[3:4], axis=-1, keepdims=True)) + LAMBDA_INIT)
    g = g_ref[...]

    n_kb = seq // B_KB
    n_qt = seq // B_TQ

    def load_qq(t):
        return _split_heads_rows(q_ref[0, pl.ds(pl.multiple_of(t * B_TQ, B_TQ), B_TQ), :])

    def scores(t, s_ref):
        qq = load_qq(t)
        cmax = None
        for kb in range(n_kb):
            s = lax.dot_general(k_ref[0, kb * B_KB:(kb + 1) * B_KB, :], qq, NT_DIMS,
                                preferred_element_type=F32)
            s_ref[kb * B_KB:(kb + 1) * B_KB, :] = s
            bmax = jnp.max(s, axis=0, keepdims=True)
            cmax = bmax if cmax is None else jnp.maximum(cmax, bmax)
        return cmax

    def softmax_pv(t, s_ref, m):
        l = jnp.zeros((1, 2 * B_TQ), F32)
        acc = jnp.zeros((2 * HEAD_DIM, 2 * B_TQ), F32)
        for c in range(seq // B_TK):
            p = jnp.exp2(s_ref[c * B_TK:(c + 1) * B_TK, :] - m)
            l = l + jnp.sum(p, axis=0, keepdims=True)
            acc = acc + jnp.dot(vt_ref[0, c], p.astype(BF16), preferred_element_type=F32)
        o = acc * (1.0 / l)
        o = o[:, :B_TQ] - lam * o[:, B_TQ:]
        o = o * lax.rsqrt(jnp.mean(o * o, axis=0, keepdims=True) + SUBLN_EPS)
        q0 = pl.multiple_of(t * B_TQ, B_TQ)
        o_ref[0, pl.ds(q0, B_TQ), :] = ((o.T * g) * (1.0 - LAMBDA_INIT)).astype(BF16)

    def tile_pair(j, m_a):
        m_b = scores(2 * j + 1, sb_ref)
        softmax_pv(2 * j, sa_ref, m_a)
        m_a = scores(2 * j + 2, sa_ref)
        softmax_pv(2 * j + 1, sb_ref, m_b)
        return m_a

    assert n_qt % 2 == 0
    m_a = lax.fori_loop(0, n_qt // 2 - 1, tile_pair, scores(0, sa_ref))
    m_b = scores(n_qt - 1, sb_ref)
    softmax_pv(n_qt - 2, sa_ref, m_a)
    softmax_pv(n_qt - 1, sb_ref, m_b)


def _diff_attention(qkb, vbt, lam_params, subln_g):
    b, s, _ = qkb.shape
    return pl.pallas_call(
        functools.partial(_diff_kernel, seq=s),
        grid=(b, B_HEADS),
        in_specs=[
            pl.BlockSpec((1, s, LANES), lambda i, h: (i, 0, h)),
            pl.BlockSpec((1, s, LANES), lambda i, h: (i, 0, B_HEADS + h)),
            pl.BlockSpec((1, s // B_TK, LANES, B_TK), lambda i, h: (i, 0, h, 0)),
            pl.BlockSpec((4, HEAD_DIM), lambda i, h: (0, 0)),
            pl.BlockSpec((1, 2 * HEAD_DIM), lambda i, h: (0, 0)),
        ],
        out_specs=pl.BlockSpec((1, s, LANES), lambda i, h: (i, 0, h)),
        out_shape=jax.ShapeDtypeStruct((b, s, B_WIDTH), BF16),
        scratch_shapes=[pltpu.VMEM((s, 2 * B_TQ), F32), pltpu.VMEM((s, 2 * B_TQ), F32)],
        compiler_params=pltpu.CompilerParams(
            dimension_semantics=("arbitrary", "arbitrary"), vmem_limit_bytes=VMEM_LIMIT_BYTES),
        name="diff_attn",
    )(qkb, qkb, vbt, lam_params, subln_g)


MERGE_TN = 512


def _split3_bf16(v):
    hi = v.astype(BF16)
    r1 = v - hi.astype(F32)
    mid = r1.astype(BF16)
    lo = (r1 - mid.astype(F32)).astype(BF16)
    return hi, mid, lo


def _merge_kernel(x_ref, g_ref, o1_ref, l1_ref, o4_ref, l4_ref, o16_ref, l16_ref, ob_ref,
                  p4_ref, p16_ref, wg_ref, wa_ref, wb_ref, wo_ref, out_ref):
    x = x_ref[0]
    tm, d = x.shape
    hb = _rms_norm(x, g_ref[...], NORM_EPS).astype(BF16)

    def token_order(o_ref_, l_ref_, p_ref_, dil):
        n = PERM // dil
        os_, ls_ = [], []
        for u in range(tm // PERM):
            xo = jnp.concatenate([o_ref_[0, r, u * n:(u + 1) * n, :] for r in range(dil)], axis=0)
            os_.append(jnp.dot(p_ref_[...], xo, preferred_element_type=F32))
            xl = jnp.concatenate([l_ref_[0, r, u * n:(u + 1) * n, :] for r in range(dil)], axis=0)
            ls_.append(sum(jnp.dot(p_ref_[...], part, preferred_element_type=F32)
                           for part in _split3_bf16(xl)))
        return jnp.concatenate(os_, axis=0), jnp.concatenate(ls_, axis=0)

    o4, l4 = token_order(o4_ref, l4_ref, p4_ref, 4)
    o16, l16 = token_order(o16_ref, l16_ref, p16_ref, 16)
    outs = [o1_ref[0, 0].astype(F32), o4, o16]
    lses = [l1_ref[0, 0], l4, l16]

    m = jnp.maximum(jnp.maximum(lses[0], lses[1]), lses[2])
    es = [jnp.exp2(v - m) for v in lses]
    num = es[0] * outs[0] + es[1] * outs[1] + es[2] * outs[2]
    attn_a = (num * (1.0 / (es[0] + es[1] + es[2]))).astype(BF16)
    attn_b = ob_ref[0]

    out = x
    for c in range(d // MERGE_TN):
        cs = slice(c * MERGE_TN, (c + 1) * MERGE_TN)
        ya = jnp.dot(attn_a, wa_ref[:, cs], preferred_element_type=F32)
        yb = jnp.dot(attn_b, wb_ref[:, cs], preferred_element_type=F32)
        ga = jnp.dot(hb, wg_ref[:, cs], preferred_element_type=F32)
        gb = jnp.dot(hb, wg_ref[:, d + c * MERGE_TN:d + (c + 1) * MERGE_TN], preferred_element_type=F32)
        merged = jax.nn.sigmoid(ga) * ya + jax.nn.sigmoid(gb) * yb
        out = out + jnp.dot(merged.astype(BF16), wo_ref[cs, :], preferred_element_type=F32)
    out_ref[0] = out


def _merge(x, g, o1, l1, o4, l4, o16, l16, ob, p4, p16, wg, wa, wb, wo):
    b, s, d = x.shape
    tm = TOKEN_TILE
    tok = lambda w: pl.BlockSpec((1, tm, w), lambda i, j: (i, j, 0))
    phase = lambda dil, w: pl.BlockSpec((1, dil, tm // dil, w), lambda i, j: (i, 0, j, 0))
    const = lambda a: pl.BlockSpec(a.shape, lambda i, j: (0,) * a.ndim)
    return pl.pallas_call(
        _merge_kernel,
        grid=(b, s // tm),
        in_specs=[tok(d), const(g), phase(1, A_WIDTH), phase(1, A_WIDTH), phase(4, A_WIDTH), phase(4, A_WIDTH),
                  phase(16, A_WIDTH), phase(16, A_WIDTH), tok(B_WIDTH), const(p4), const(p16),
                  const(wg), const(wa), const(wb), const(wo)],
        out_specs=tok(d),
        out_shape=jax.ShapeDtypeStruct((b, s, d), F32),
        compiler_params=pltpu.CompilerParams(
            dimension_semantics=("arbitrary", "arbitrary"), vmem_limit_bytes=VMEM_LIMIT_BYTES),
        name="merge_proj",
    )(x, g, o1, l1, o4, l4, o16, l16, ob, p4, p16, wg, wa, wb, wo)


MLP_TF = 512


def _mlp_kernel(x_ref, g_ref, w1_ref, w2_ref, gf_ref, out_ref):
    x = x_ref[...]
    hb = _rms_norm(x, g_ref[...], NORM_EPS).astype(BF16)
    acc = x
    for c in range(w1_ref.shape[1] // MLP_TF):
        cs = slice(c * MLP_TF, (c + 1) * MLP_TF)
        u = jnp.maximum(jnp.dot(hb, w1_ref[:, cs], preferred_element_type=F32), 0.0)
        acc = acc + jnp.dot((u * u).astype(BF16), w2_ref[cs, :], preferred_element_type=F32)
    out_ref[...] = _rms_norm(acc, gf_ref[...], NORM_EPS)


def _mlp(x2d, g, w1, w2, gf):
    t, d = x2d.shape
    tm = TOKEN_TILE
    row = pl.BlockSpec((tm, d), lambda i: (i, 0))
    full = lambda a: pl.BlockSpec(a.shape, lambda i: (0, 0))
    return pl.pallas_call(
        _mlp_kernel,
        grid=(t // tm,),
        in_specs=[row, full(g), full(w1), full(w2), full(gf)],
        out_specs=row,
        out_shape=jax.ShapeDtypeStruct((t, d), F32),
        compiler_params=pltpu.CompilerParams(
            dimension_semantics=("arbitrary",), vmem_limit_bytes=VMEM_LIMIT_BYTES),
        name="mlp_norm",
    )(x2d, g, w1, w2, gf)


def _rope_tables(seq):
    pos = jnp.arange(seq, dtype=F32)
    inv_freq = ROPE_THETA ** (-jnp.arange(0, HEAD_DIM, 2, dtype=F32) / HEAD_DIM)
    ang = pos[:, None] * inv_freq[None, :]
    cos = jnp.cos(ang)
    sin = jnp.sin(ang)
    cos = jnp.concatenate([cos, cos, cos, cos], axis=-1)
    sin = jnp.concatenate([-sin, sin, -sin, sin], axis=-1)
    return cos, sin


def kernel(x, w_in, w_branch_a, w_branch_b, w_out, lambda_q1, lambda_k1, lambda_q2, lambda_k2,
           diff_subln_g, norm_mix_g, norm_mlp_g, w_ff1, w_ff2, norm_final_g):
    b, s, d = x.shape
    assert w_in.shape[0] == 1, "single-layer block: the final norm is fused into the MLP kernel"
    assert s % B_KB == 0 and s % TOKEN_TILE == 0
    cos, sin = _rope_tables(s)
    w_in_l = w_in[0].astype(BF16)
    g_mix = norm_mix_g[0].reshape(1, d)
    vb_lo = QKV_A_WIDTH + QK_B_WIDTH
    p4 = jnp.asarray(_phase_perm(4), BF16)
    p16 = jnp.asarray(_phase_perm(16), BF16)
    qkva, qkb, vbt, ph4, ph16 = _qkv_proj(
        x, g_mix, w_in_l[:, :vb_lo], w_in_l[:, vb_lo:QKV_WIDTH].T, cos, sin, p4, p16)
    o1, l1 = _band_attention(qkva.reshape(b, 1, s, QKV_A_WIDTH), DILATED_PATTERNS[0][0])
    o4, l4 = _band_attention(ph4, DILATED_PATTERNS[1][0])
    o16, l16 = _band_attention(ph16, DILATED_PATTERNS[2][0])
    lam_params = jnp.stack([lambda_q1[0], lambda_k1[0], lambda_q2[0], lambda_k2[0]]).astype(F32)
    ob = _diff_attention(qkb, vbt, lam_params, diff_subln_g[0].reshape(1, 2 * HEAD_DIM).astype(F32))
    x1 = _merge(x, g_mix, o1, l1, o4, l4, o16, l16, ob, p4.T, p16.T, w_in_l[:, QKV_WIDTH:],
                w_branch_a[0].astype(BF16), w_branch_b[0].astype(BF16), w_out[0].astype(BF16))
    out = _mlp(x1.reshape(b * s, d), norm_mlp_g[0].reshape(1, d), w_ff1[0].astype(BF16),
               w_ff2[0].astype(BF16), norm_final_g.reshape(1, d))
    return out.reshape(b, s, d)
```

```python
import functools
import math

import numpy as np
import jax
import jax.numpy as jnp
from jax import lax
from jax.experimental import pallas as pl
from jax.experimental.pallas import tpu as pltpu

HEAD_DIM = 64
HALF_DIM = HEAD_DIM // 2
LANES = 128
A_HEADS = 8
A_WIDTH = A_HEADS * HEAD_DIM
B_HEADS = 4
B_WIDTH = B_HEADS * 2 * HEAD_DIM
QKV_A_WIDTH = 3 * A_WIDTH
QK_B_WIDTH = 2 * B_WIDTH
QKV_WIDTH = QKV_A_WIDTH + QK_B_WIDTH + B_WIDTH
DILATED_PATTERNS = ((128, 1), (512, 4), (2048, 16))
ROPE_THETA = 10000.0
NORM_EPS = 1e-6
SUBLN_EPS = 1e-5
LAMBDA_INIT = 0.8 - 0.6 * math.exp(-0.3 * 0)
QK_SCALE_LOG2E = math.log2(math.e) / math.sqrt(HEAD_DIM)
MASK_VALUE = -1e30
VMEM_LIMIT_BYTES = 56 * 1024 * 1024
PERM = 256
TOKEN_TILE = 512

BF16 = jnp.bfloat16
F32 = jnp.float32
NT_DIMS = (((1,), (1,)), ((), ()))


def _rms_norm(x, g, eps):
    return (x * lax.rsqrt(jnp.mean(x * x, axis=-1, keepdims=True) + eps)) * g


def _split_heads_rows(q):
    lane = lax.broadcasted_iota(jnp.int32, q.shape, 1)
    zero = jnp.zeros_like(q)
    return jnp.concatenate(
        [jnp.where(lane < HEAD_DIM, q, zero), jnp.where(lane >= HEAD_DIM, q, zero)], axis=0)


def _phase_perm(dil):
    n = PERM // dil
    p = np.zeros((PERM, PERM), np.float32)
    for r in range(dil):
        for i in range(n):
            p[r * n + i, i * dil + r] = 1.0
    return p


def _qkv_kernel(x_ref, g_ref, w_ref, wvt_ref, cos_ref, sin_ref, p4_ref, p16_ref,
                qkva_ref, qkb_ref, vbt_ref, ph4_ref, ph16_ref):
    x = x_ref[0]
    ts = x.shape[0]
    hb = _rms_norm(x, g_ref[...], NORM_EPS).astype(BF16)
    cos = cos_ref[...]
    sin = sin_ref[...]
    lane = lax.broadcasted_iota(jnp.int32, cos.shape, 1)
    first_half = (lane % HEAD_DIM) < HALF_DIM

    def rope(t, scale):
        parts = []
        for j in range(t.shape[1] // LANES):
            tj = t[:, j * LANES:(j + 1) * LANES]
            rot = jnp.where(first_half, pltpu.roll(tj, LANES - HALF_DIM, 1), pltpu.roll(tj, HALF_DIM, 1))
            tj = tj * cos + rot * sin
            parts.append((tj * scale if scale != 1.0 else tj).astype(BF16))
        return parts

    for c in range(5):
        t = jnp.dot(hb, w_ref[:, c * A_WIDTH:(c + 1) * A_WIDTH], preferred_element_type=F32)
        dst, base = (qkva_ref, c * A_WIDTH) if c < 3 else (qkb_ref, (c - 3) * A_WIDTH)
        if c == 2:
            dst[0, :, base:base + A_WIDTH] = t.astype(BF16)
        else:
            scale = QK_SCALE_LOG2E if c in (0, 3) else 1.0
            for j, tj in enumerate(rope(t, scale)):
                dst[0, :, base + j * LANES:base + (j + 1) * LANES] = tj
    vbt = lax.dot_general(wvt_ref[...], hb, NT_DIMS, preferred_element_type=F32).astype(BF16)
    for u in range(ts // B_TK):
        vbt_ref[0, u] = vbt[:, u * B_TK:(u + 1) * B_TK]

    for dil, p_ref, ph_ref in ((4, p4_ref, ph4_ref), (16, p16_ref, ph16_ref)):
        n = PERM // dil
        for u in range(ts // PERM):
            for c in range(3):
                cs = slice(c * A_WIDTH, (c + 1) * A_WIDTH)
                tb = qkva_ref[0, u * PERM:(u + 1) * PERM, cs]
                ph = jnp.dot(p_ref[...], tb, preferred_element_type=F32).astype(BF16)
                for r in range(dil):
                    ph_ref[0, r, u * n:(u + 1) * n, cs] = ph[r * n:(r + 1) * n]


def _qkv_proj(x, g, w_qkv, w_vbt, cos, sin, p4, p16):
    b, s, d = x.shape
    ts = TOKEN_TILE
    const = lambda a: pl.BlockSpec(a.shape, lambda i, j: (0,) * a.ndim)
    return pl.pallas_call(
        _qkv_kernel,
        grid=(b, s // ts),
        in_specs=[
            pl.BlockSpec((1, ts, d), lambda i, j: (i, j, 0)),
            const(g), const(w_qkv), const(w_vbt),
            pl.BlockSpec((ts, LANES), lambda i, j: (j, 0)),
            pl.BlockSpec((ts, LANES), lambda i, j: (j, 0)),
            const(p4), const(p16),
        ],
        out_specs=[
            pl.BlockSpec((1, ts, QKV_A_WIDTH), lambda i, j: (i, j, 0)),
            pl.BlockSpec((1, ts, QK_B_WIDTH), lambda i, j: (i, j, 0)),
            pl.BlockSpec((1, ts // B_TK, B_WIDTH, B_TK), lambda i, j: (i, j, 0, 0)),
            pl.BlockSpec((1, 4, ts // 4, QKV_A_WIDTH), lambda i, j: (i, 0, j, 0)),
            pl.BlockSpec((1, 16, ts // 16, QKV_A_WIDTH), lambda i, j: (i, 0, j, 0)),
        ],
        out_shape=[
            jax.ShapeDtypeStruct((b, s, QKV_A_WIDTH), BF16),
            jax.ShapeDtypeStruct((b, s, QK_B_WIDTH), BF16),
            jax.ShapeDtypeStruct((b, s // B_TK, B_WIDTH, B_TK), BF16),
            jax.ShapeDtypeStruct((b, 4, s // 4, QKV_A_WIDTH), BF16),
            jax.ShapeDtypeStruct((b, 16, s // 16, QKV_A_WIDTH), BF16),
        ],
        compiler_params=pltpu.CompilerParams(
            dimension_semantics=("arbitrary", "arbitrary"), vmem_limit_bytes=VMEM_LIMIT_BYTES),
        name="qkv_proj",
    )(x, g, w_qkv, w_vbt, cos, sin, p4, p16)


A_TQ = 128
A_WIN = 256
BAND_ROWS = 1024


def _band_bias():
    row = np.arange(2 * A_TQ)[:, None] % A_TQ
    col = np.arange(A_WIN)[None, :]
    half = (A_WIN - A_TQ) // 2
    return np.stack([np.where(np.abs(col - row + d) <= half, 0.0, MASK_VALUE)
                     for d in (-half, 0, -2 * half)]).astype(np.float32)


def _band_kernel(q_ref, k_ref, v_ref, bias_ref, o_ref, lse_ref, *, half, ls, rows_per_step):
    step = pl.program_id(2)
    n_sub = rows_per_step // A_TQ
    n_phase = q_ref.shape[1]
    out_lane = lax.broadcasted_iota(jnp.int32, (A_TQ, LANES), 1)

    def sub_tile(u, carry):
        ph = u // n_sub
        r0 = pl.multiple_of((u % n_sub) * A_TQ, A_TQ)
        t0 = step * rows_per_step + r0
        start = pl.multiple_of(jnp.clip(t0 - half, 0, ls - A_WIN), half)
        which = jnp.where(t0 == 0, 1, jnp.where(t0 == ls - A_TQ, 2, 0))
        for j in range(A_WIDTH // LANES):
            cs = slice(j * LANES, (j + 1) * LANES)
            qq = _split_heads_rows(q_ref[0, ph, pl.ds(r0, A_TQ), cs])
            kw = k_ref[0, ph, pl.ds(start, A_WIN), cs]
            vw = v_ref[0, ph, pl.ds(start, A_WIN), cs]
            s = lax.dot_general(qq, kw, NT_DIMS, preferred_element_type=F32) + bias_ref[which]
            m = jnp.max(s, axis=-1, keepdims=True)
            p = jnp.exp2(s - m)
            l = jnp.sum(p, axis=-1, keepdims=True)
            pv = jnp.dot(p.astype(BF16), vw, preferred_element_type=F32) * (1.0 / l)
            o_ref[0, ph, pl.ds(r0, A_TQ), cs] = jnp.where(
                out_lane < HEAD_DIM, pv[:A_TQ], pv[A_TQ:]).astype(BF16)
            lse = m + jnp.log2(l)
            lse_ref[0, ph, pl.ds(r0, A_TQ), cs] = jnp.where(out_lane < HEAD_DIM, lse[:A_TQ], lse[A_TQ:])
        return carry

    lax.fori_loop(0, n_phase * n_sub, sub_tile, 0, unroll=4)


def _band_attention(ph, window):
    b, dil, ls, _ = ph.shape
    half = window // (2 * dil)
    assert half == A_TQ // 2 and A_WIN == A_TQ + 2 * half and ls % A_TQ == 0 and ls >= A_WIN
    rows_per_step = min(ls, BAND_ROWS)
    n_phase = BAND_ROWS // rows_per_step
    assert dil % n_phase == 0
    kern = functools.partial(_band_kernel, half=half, ls=ls, rows_per_step=rows_per_step)
    return pl.pallas_call(
        kern,
        grid=(b, dil // n_phase, ls // rows_per_step),
        in_specs=[
            pl.BlockSpec((1, n_phase, rows_per_step, A_WIDTH), lambda i, r, t: (i, r, t, 0)),
            pl.BlockSpec((1, n_phase, ls, A_WIDTH), lambda i, r, t: (i, r, 0, 1)),
            pl.BlockSpec((1, n_phase, ls, A_WIDTH), lambda i, r, t: (i, r, 0, 2)),
            pl.BlockSpec((3, 2 * A_TQ, A_WIN), lambda i, r, t: (0, 0, 0)),
        ],
        out_specs=[
            pl.BlockSpec((1, n_phase, rows_per_step, A_WIDTH), lambda i, r, t: (i, r, t, 0)),
            pl.BlockSpec((1, n_phase, rows_per_step, A_WIDTH), lambda i, r, t: (i, r, t, 0)),
        ],
        out_shape=[
            jax.ShapeDtypeStruct((b, dil, ls, A_WIDTH), BF16),
            jax.ShapeDtypeStruct((b, dil, ls, A_WIDTH), F32),
        ],
        compiler_params=pltpu.CompilerParams(
            dimension_semantics=("arbitrary", "arbitrary", "arbitrary"),
            vmem_limit_bytes=VMEM_LIMIT_BYTES),
        name=f"band_attn_d{dil}",
    )(ph, ph, ph, jnp.asarray(_band_bias()))


B_TQ = 256
B_KB = 1024
B_TK = 256


def _diff_kernel(q_ref, k_ref, vt_ref, lam_ref, g_ref, o_ref, sa_ref, sb_ref, *, seq):
    lam_p = lam_ref[...]
    lam = (jnp.exp(jnp.sum(lam_p[0:1] * lam_p[1:2], axis=-1, keepdims=True))
           - jnp.exp(jnp.sum(lam_p[2:3] * lam_p[3:4], axis=-1, keepdims=True)) + LAMBDA_INIT)
    g = g_ref[...]

    n_kb = seq // B_KB
    n_qt = seq // B_TQ

    def load_qq(t):
        return _split_heads_rows(q_ref[0, pl.ds(pl.multiple_of(t * B_TQ, B_TQ), B_TQ), :])

    def scores(t, s_ref):
        qq = load_qq(t)
        cmax = None
        for kb in range(n_kb):
            s = lax.dot_general(k_ref[0, kb * B_KB:(kb + 1) * B_KB, :], qq, NT_DIMS,
                                preferred_element_type=F32)
            s_ref[kb * B_KB:(kb + 1) * B_KB, :] = s
            bmax = jnp.max(s, axis=0, keepdims=True)
            cmax = bmax if cmax is None else jnp.maximum(cmax, bmax)
        return cmax

    def softmax_pv(s_ref, m):
        l = jnp.zeros((1, 2 * B_TQ), F32)
        acc = jnp.zeros((2 * HEAD_DIM, 2 * B_TQ), F32)
        for c in range(seq // B_TK):
            p = jnp.exp2(s_ref[c * B_TK:(c + 1) * B_TK, :] - m)
            l = l + jnp.sum(p, axis=0, keepdims=True)
            acc = acc + jnp.dot(vt_ref[0, c], p.astype(BF16), preferred_element_type=F32)
        return l, acc

    def finish(t, l_acc):
        l, acc = l_acc
        o = acc * (1.0 / l)
        o = o[:, :B_TQ] - lam * o[:, B_TQ:]
        o = o * lax.rsqrt(jnp.mean(o * o, axis=0, keepdims=True) + SUBLN_EPS)
        q0 = pl.multiple_of(t * B_TQ, B_TQ)
        o_ref[0, pl.ds(q0, B_TQ), :] = ((o.T * g) * (1.0 - LAMBDA_INIT)).astype(BF16)

    def tile_pair(j, carry):
        m_a, pending = carry
        m_b = scores(2 * j + 1, sb_ref)
        finish(jnp.maximum(2 * j - 1, 0), pending)
        pending = softmax_pv(sa_ref, m_a)
        m_a = scores(2 * j + 2, sa_ref)
        finish(2 * j, pending)
        return m_a, softmax_pv(sb_ref, m_b)

    assert n_qt % 2 == 0 and n_qt >= 4
    placeholder = (jnp.ones((1, 2 * B_TQ), F32), jnp.zeros((2 * HEAD_DIM, 2 * B_TQ), F32))
    m_a, pending = lax.fori_loop(0, n_qt // 2 - 1, tile_pair, (scores(0, sa_ref), placeholder))
    m_b = scores(n_qt - 1, sb_ref)
    finish(n_qt - 3, pending)
    finish(n_qt - 2, softmax_pv(sa_ref, m_a))
    finish(n_qt - 1, softmax_pv(sb_ref, m_b))


def _diff_attention(qkb, vbt, lam_params, subln_g):
    b, s, _ = qkb.shape
    return pl.pallas_call(
        functools.partial(_diff_kernel, seq=s),
        grid=(b, B_HEADS),
        in_specs=[
            pl.BlockSpec((1, s, LANES), lambda i, h: (i, 0, h)),
            pl.BlockSpec((1, s, LANES), lambda i, h: (i, 0, B_HEADS + h)),
            pl.BlockSpec((1, s // B_TK, LANES, B_TK), lambda i, h: (i, 0, h, 0)),
            pl.BlockSpec((4, HEAD_DIM), lambda i, h: (0, 0)),
            pl.BlockSpec((1, 2 * HEAD_DIM), lambda i, h: (0, 0)),
        ],
        out_specs=pl.BlockSpec((1, s, LANES), lambda i, h: (i, 0, h)),
        out_shape=jax.ShapeDtypeStruct((b, s, B_WIDTH), BF16),
        scratch_shapes=[pltpu.VMEM((s, 2 * B_TQ), F32), pltpu.VMEM((s, 2 * B_TQ), F32)],
        compiler_params=pltpu.CompilerParams(
            dimension_semantics=("arbitrary", "arbitrary"), vmem_limit_bytes=VMEM_LIMIT_BYTES),
        name="diff_attn",
    )(qkb, qkb, vbt, lam_params, subln_g)


MERGE_TN = 512


def _split3_bf16(v):
    hi = v.astype(BF16)
    r1 = v - hi.astype(F32)
    mid = r1.astype(BF16)
    lo = (r1 - mid.astype(F32)).astype(BF16)
    return hi, mid, lo


def _merge_kernel(x_ref, g_ref, o1_ref, l1_ref, o4_ref, l4_ref, o16_ref, l16_ref, ob_ref,
                  p4_ref, p16_ref, wg_ref, wa_ref, wb_ref, wo_ref, out_ref):
    x = x_ref[0]
    tm, d = x.shape
    hb = _rms_norm(x, g_ref[...], NORM_EPS).astype(BF16)

    def token_order(o_ref_, l_ref_, p_ref_, dil):
        n = PERM // dil
        os_, ls_ = [], []
        for u in range(tm // PERM):
            xo = jnp.concatenate([o_ref_[0, r, u * n:(u + 1) * n, :] for r in range(dil)], axis=0)
            os_.append(jnp.dot(p_ref_[...], xo, preferred_element_type=F32))
            xl = jnp.concatenate([l_ref_[0, r, u * n:(u + 1) * n, :] for r in range(dil)], axis=0)
            ls_.append(sum(jnp.dot(p_ref_[...], part, preferred_element_type=F32)
                           for part in _split3_bf16(xl)))
        return jnp.concatenate(os_, axis=0), jnp.concatenate(ls_, axis=0)

    o4, l4 = token_order(o4_ref, l4_ref, p4_ref, 4)
    o16, l16 = token_order(o16_ref, l16_ref, p16_ref, 16)
    outs = [o1_ref[0, 0].astype(F32), o4, o16]
    lses = [l1_ref[0, 0], l4, l16]

    m = jnp.maximum(jnp.maximum(lses[0], lses[1]), lses[2])
    es = [jnp.exp2(v - m) for v in lses]
    num = es[0] * outs[0] + es[1] * outs[1] + es[2] * outs[2]
    attn_a = (num * (1.0 / (es[0] + es[1] + es[2]))).astype(BF16)
    attn_b = ob_ref[0]

    out = x
    for c in range(d // MERGE_TN):
        cs = slice(c * MERGE_TN, (c + 1) * MERGE_TN)
        ya = jnp.dot(attn_a, wa_ref[:, cs], preferred_element_type=F32)
        yb = jnp.dot(attn_b, wb_ref[:, cs], preferred_element_type=F32)
        ga = jnp.dot(hb, wg_ref[:, cs], preferred_element_type=F32)
        gb = jnp.dot(hb, wg_ref[:, d + c * MERGE_TN:d + (c + 1) * MERGE_TN], preferred_element_type=F32)
        merged = jax.nn.sigmoid(ga) * ya + jax.nn.sigmoid(gb) * yb
        out = out + jnp.dot(merged.astype(BF16), wo_ref[cs, :], preferred_element_type=F32)
    out_ref[0] = out


def _merge(x, g, o1, l1, o4, l4, o16, l16, ob, p4, p16, wg, wa, wb, wo):
    b, s, d = x.shape
    tm = TOKEN_TILE
    tok = lambda w: pl.BlockSpec((1, tm, w), lambda i, j: (i, j, 0))
    phase = lambda dil, w: pl.BlockSpec((1, dil, tm // dil, w), lambda i, j: (i, 0, j, 0))
    const = lambda a: pl.BlockSpec(a.shape, lambda i, j: (0,) * a.ndim)
    return pl.pallas_call(
        _merge_kernel,
        grid=(b, s // tm),
        in_specs=[tok(d), const(g), phase(1, A_WIDTH), phase(1, A_WIDTH), phase(4, A_WIDTH), phase(4, A_WIDTH),
                  phase(16, A_WIDTH), phase(16, A_WIDTH), tok(B_WIDTH), const(p4), const(p16),
                  const(wg), const(wa), const(wb), const(wo)],
        out_specs=tok(d),
        out_shape=jax.ShapeDtypeStruct((b, s, d), F32),
        compiler_params=pltpu.CompilerParams(
            dimension_semantics=("arbitrary", "arbitrary"), vmem_limit_bytes=VMEM_LIMIT_BYTES),
        name="merge_proj",
    )(x, g, o1, l1, o4, l4, o16, l16, ob, p4, p16, wg, wa, wb, wo)


MLP_TF = 512


def _mlp_kernel(x_ref, g_ref, w1_ref, w2_ref, gf_ref, out_ref):
    x = x_ref[...]
    hb = _rms_norm(x, g_ref[...], NORM_EPS).astype(BF16)
    acc = x
    for c in range(w1_ref.shape[1] // MLP_TF):
        cs = slice(c * MLP_TF, (c + 1) * MLP_TF)
        u = jnp.maximum(jnp.dot(hb, w1_ref[:, cs], preferred_element_type=F32), 0.0)
        acc = acc + jnp.dot((u * u).astype(BF16), w2_ref[cs, :], preferred_element_type=F32)
    out_ref[...] = _rms_norm(acc, gf_ref[...], NORM_EPS)


def _mlp(x2d, g, w1, w2, gf):
    t, d = x2d.shape
    tm = TOKEN_TILE
    row = pl.BlockSpec((tm, d), lambda i: (i, 0))
    full = lambda a: pl.BlockSpec(a.shape, lambda i: (0, 0))
    return pl.pallas_call(
        _mlp_kernel,
        grid=(t // tm,),
        in_specs=[row, full(g), full(w1), full(w2), full(gf)],
        out_specs=row,
        out_shape=jax.ShapeDtypeStruct((t, d), F32),
        compiler_params=pltpu.CompilerParams(
            dimension_semantics=("arbitrary",), vmem_limit_bytes=VMEM_LIMIT_BYTES),
        name="mlp_norm",
    )(x2d, g, w1, w2, gf)


def _rope_tables(seq):
    pos = jnp.arange(seq, dtype=F32)
    inv_freq = ROPE_THETA ** (-jnp.arange(0, HEAD_DIM, 2, dtype=F32) / HEAD_DIM)
    ang = pos[:, None] * inv_freq[None, :]
    cos = jnp.cos(ang)
    sin = jnp.sin(ang)
    cos = jnp.concatenate([cos, cos, cos, cos], axis=-1)
    sin = jnp.concatenate([-sin, sin, -sin, sin], axis=-1)
    return cos, sin


def kernel(x, w_in, w_branch_a, w_branch_b, w_out, lambda_q1, lambda_k1, lambda_q2, lambda_k2,
           diff_subln_g, norm_mix_g, norm_mlp_g, w_ff1, w_ff2, norm_final_g):
    b, s, d = x.shape
    assert w_in.shape[0] == 1, "single-layer block: the final norm is fused into the MLP kernel"
    assert s % B_KB == 0 and s % TOKEN_TILE == 0
    cos, sin = _rope_tables(s)
    w_in_l = w_in[0].astype(BF16)
    g_mix = norm_mix_g[0].reshape(1, d)
    vb_lo = QKV_A_WIDTH + QK_B_WIDTH
    p4 = jnp.asarray(_phase_perm(4), BF16)
    p16 = jnp.asarray(_phase_perm(16), BF16)
    qkva, qkb, vbt, ph4, ph16 = _qkv_proj(
        x, g_mix, w_in_l[:, :vb_lo], w_in_l[:, vb_lo:QKV_WIDTH].T, cos, sin, p4, p16)
    o1, l1 = _band_attention(qkva.reshape(b, 1, s, QKV_A_WIDTH), DILATED_PATTERNS[0][0])
    o4, l4 = _band_attention(ph4, DILATED_PATTERNS[1][0])
    o16, l16 = _band_attention(ph16, DILATED_PATTERNS[2][0])
    lam_params = jnp.stack([lambda_q1[0], lambda_k1[0], lambda_q2[0], lambda_k2[0]]).astype(F32)
    ob = _diff_attention(qkb, vbt, lam_params, diff_subln_g[0].reshape(1, 2 * HEAD_DIM).astype(F32))
    x1 = _merge(x, g_mix, o1, l1, o4, l4, o16, l16, ob, p4.T, p16.T, w_in_l[:, QKV_WIDTH:],
                w_branch_a[0].astype(BF16), w_branch_b[0].astype(BF16), w_out[0].astype(BF16))
    out = _mlp(x1.reshape(b * s, d), norm_mlp_g[0].reshape(1, d), w_ff1[0].astype(BF16),
               w_ff2[0].astype(BF16), norm_final_g.reshape(1, d))
    return out.reshape(b, s, d)
```

```python
import functools
import math

import numpy as np
import jax
import jax.numpy as jnp
from jax import lax
from jax.experimental import pallas as pl
from jax.experimental.pallas import tpu as pltpu

HEAD_DIM = 64
HALF_DIM = HEAD_DIM // 2
LANES = 128
A_HEADS = 8
A_WIDTH = A_HEADS * HEAD_DIM
B_HEADS = 4
B_WIDTH = B_HEADS * 2 * HEAD_DIM
QKV_A_WIDTH = 3 * A_WIDTH
QK_B_WIDTH = 2 * B_WIDTH
QKV_WIDTH = QKV_A_WIDTH + QK_B_WIDTH + B_WIDTH
DILATED_PATTERNS = ((128, 1), (512, 4), (2048, 16))
ROPE_THETA = 10000.0
NORM_EPS = 1e-6
SUBLN_EPS = 1e-5
LAMBDA_INIT = 0.8 - 0.6 * math.exp(-0.3 * 0)
QK_SCALE_LOG2E = math.log2(math.e) / math.sqrt(HEAD_DIM)
MASK_VALUE = -1e30
VMEM_LIMIT_BYTES = 56 * 1024 * 1024
PERM = 256
TOKEN_TILE = 512
QKV_TS = 1024

BF16 = jnp.bfloat16
F32 = jnp.float32
NT_DIMS = (((1,), (1,)), ((), ()))


def _rms_norm(x, g, eps):
    return (x * lax.rsqrt(jnp.mean(x * x, axis=-1, keepdims=True) + eps)) * g


def _split_heads_rows(q):
    lane = lax.broadcasted_iota(jnp.int32, q.shape, 1)
    zero = jnp.zeros_like(q)
    return jnp.concatenate(
        [jnp.where(lane < HEAD_DIM, q, zero), jnp.where(lane >= HEAD_DIM, q, zero)], axis=0)


def _phase_perm(dil):
    n = PERM // dil
    p = np.zeros((PERM, PERM), np.float32)
    for r in range(dil):
        for i in range(n):
            p[r * n + i, i * dil + r] = 1.0
    return p


def _qkv_kernel(x_ref, g_ref, w_ref, wvt_ref, cos_ref, sin_ref, p4_ref, p16_ref,
                qkva_ref, qkb_ref, vbt_ref, ph4_ref, ph16_ref):
    x = x_ref[0]
    ts = x.shape[0]
    hb = _rms_norm(x, g_ref[...], NORM_EPS).astype(BF16)
    cos = cos_ref[...]
    sin = sin_ref[...]
    lane = lax.broadcasted_iota(jnp.int32, cos.shape, 1)
    first_half = (lane % HEAD_DIM) < HALF_DIM

    def rope(t, scale):
        parts = []
        for j in range(t.shape[1] // LANES):
            tj = t[:, j * LANES:(j + 1) * LANES]
            rot = jnp.where(first_half, pltpu.roll(tj, LANES - HALF_DIM, 1), pltpu.roll(tj, HALF_DIM, 1))
            tj = tj * cos + rot * sin
            parts.append((tj * scale if scale != 1.0 else tj).astype(BF16))
        return parts

    for c in range(5):
        t = jnp.dot(hb, w_ref[:, c * A_WIDTH:(c + 1) * A_WIDTH], preferred_element_type=F32)
        dst, base = (qkva_ref, c * A_WIDTH) if c < 3 else (qkb_ref, (c - 3) * A_WIDTH)
        if c == 2:
            dst[0, :, base:base + A_WIDTH] = t.astype(BF16)
        else:
            scale = QK_SCALE_LOG2E if c in (0, 3) else 1.0
            for j, tj in enumerate(rope(t, scale)):
                dst[0, :, base + j * LANES:base + (j + 1) * LANES] = tj
    vbt = lax.dot_general(wvt_ref[...], hb, NT_DIMS, preferred_element_type=F32).astype(BF16)
    for u in range(ts // B_TK):
        vbt_ref[0, u] = vbt[:, u * B_TK:(u + 1) * B_TK]

    for dil, p_ref, ph_ref in ((4, p4_ref, ph4_ref), (16, p16_ref, ph16_ref)):
        n = PERM // dil
        for u in range(ts // PERM):
            for c in range(3):
                cs = slice(c * A_WIDTH, (c + 1) * A_WIDTH)
                tb = qkva_ref[0, u * PERM:(u + 1) * PERM, cs]
                ph = jnp.dot(p_ref[...], tb, preferred_element_type=F32).astype(BF16)
                for r in range(dil):
                    ph_ref[0, r, u * n:(u + 1) * n, cs] = ph[r * n:(r + 1) * n]


def _qkv_proj(x, g, w_qkv, w_vbt, cos, sin, p4, p16):
    b, s, d = x.shape
    ts = QKV_TS
    const = lambda a: pl.BlockSpec(a.shape, lambda i, j: (0,) * a.ndim, pipeline_mode=pl.Buffered(1))
    return pl.pallas_call(
        _qkv_kernel,
        grid=(b, s // ts),
        in_specs=[
            pl.BlockSpec((1, ts, d), lambda i, j: (i, j, 0)),
            const(g), const(w_qkv), const(w_vbt),
            pl.BlockSpec((ts, LANES), lambda i, j: (j, 0)),
            pl.BlockSpec((ts, LANES), lambda i, j: (j, 0)),
            const(p4), const(p16),
        ],
        out_specs=[
            pl.BlockSpec((1, ts, QKV_A_WIDTH), lambda i, j: (i, j, 0)),
            pl.BlockSpec((1, ts, QK_B_WIDTH), lambda i, j: (i, j, 0)),
            pl.BlockSpec((1, ts // B_TK, B_WIDTH, B_TK), lambda i, j: (i, j, 0, 0)),
            pl.BlockSpec((1, 4, ts // 4, QKV_A_WIDTH), lambda i, j: (i, 0, j, 0)),
            pl.BlockSpec((1, 16, ts // 16, QKV_A_WIDTH), lambda i, j: (i, 0, j, 0)),
        ],
        out_shape=[
            jax.ShapeDtypeStruct((b, s, QKV_A_WIDTH), BF16),
            jax.ShapeDtypeStruct((b, s, QK_B_WIDTH), BF16),
            jax.ShapeDtypeStruct((b, s // B_TK, B_WIDTH, B_TK), BF16),
            jax.ShapeDtypeStruct((b, 4, s // 4, QKV_A_WIDTH), BF16),
            jax.ShapeDtypeStruct((b, 16, s // 16, QKV_A_WIDTH), BF16),
        ],
        compiler_params=pltpu.CompilerParams(
            dimension_semantics=("arbitrary", "arbitrary"), vmem_limit_bytes=VMEM_LIMIT_BYTES),
        name="qkv_proj",
    )(x, g, w_qkv, w_vbt, cos, sin, p4, p16)


A_TQ = 128
A_WIN = 256
BAND_ROWS = 1024


def _band_bias():
    row = np.arange(2 * A_TQ)[:, None] % A_TQ
    col = np.arange(A_WIN)[None, :]
    half = (A_WIN - A_TQ) // 2
    return np.stack([np.where(np.abs(col - row + d) <= half, 0.0, MASK_VALUE)
                     for d in (-half, 0, -2 * half)]).astype(np.float32)


def _band_kernel(q_ref, k_ref, v_ref, bias_ref, o_ref, lse_ref, *, half, ls, rows_per_step):
    step = pl.program_id(2)
    n_sub = rows_per_step // A_TQ
    n_phase = q_ref.shape[1]
    out_lane = lax.broadcasted_iota(jnp.int32, (A_TQ, LANES), 1)

    def sub_tile(u, carry):
        ph = u // n_sub
        r0 = pl.multiple_of((u % n_sub) * A_TQ, A_TQ)
        t0 = step * rows_per_step + r0
        start = pl.multiple_of(jnp.clip(t0 - half, 0, ls - A_WIN), half)
        which = jnp.where(t0 == 0, 1, jnp.where(t0 == ls - A_TQ, 2, 0))
        for j in range(A_WIDTH // LANES):
            cs = slice(j * LANES, (j + 1) * LANES)
            qq = _split_heads_rows(q_ref[0, ph, pl.ds(r0, A_TQ), cs])
            kw = k_ref[0, ph, pl.ds(start, A_WIN), cs]
            vw = v_ref[0, ph, pl.ds(start, A_WIN), cs]
            s = lax.dot_general(qq, kw, NT_DIMS, preferred_element_type=F32) + bias_ref[which]
            m = jnp.max(s, axis=-1, keepdims=True)
            p = jnp.exp2(s - m)
            l = jnp.sum(p, axis=-1, keepdims=True)
            pv = jnp.dot(p.astype(BF16), vw, preferred_element_type=F32) * (1.0 / l)
            o_ref[0, ph, pl.ds(r0, A_TQ), cs] = jnp.where(
                out_lane < HEAD_DIM, pv[:A_TQ], pv[A_TQ:]).astype(BF16)
            lse = m + jnp.log2(l)
            lse_ref[0, ph, pl.ds(r0, A_TQ), cs] = jnp.where(out_lane < HEAD_DIM, lse[:A_TQ], lse[A_TQ:])
        return carry

    lax.fori_loop(0, n_phase * n_sub, sub_tile, 0, unroll=4)


def _band_attention(ph, window):
    b, dil, ls, _ = ph.shape
    half = window // (2 * dil)
    assert half == A_TQ // 2 and A_WIN == A_TQ + 2 * half and ls % A_TQ == 0 and ls >= A_WIN
    rows_per_step = min(ls, BAND_ROWS)
    n_phase = BAND_ROWS // rows_per_step
    assert dil % n_phase == 0
    kern = functools.partial(_band_kernel, half=half, ls=ls, rows_per_step=rows_per_step)
    return pl.pallas_call(
        kern,
        grid=(b, dil // n_phase, ls // rows_per_step),
        in_specs=[
            pl.BlockSpec((1, n_phase, rows_per_step, A_WIDTH), lambda i, r, t: (i, r, t, 0)),
            pl.BlockSpec((1, n_phase, ls, A_WIDTH), lambda i, r, t: (i, r, 0, 1)),
            pl.BlockSpec((1, n_phase, ls, A_WIDTH), lambda i, r, t: (i, r, 0, 2)),
            pl.BlockSpec((3, 2 * A_TQ, A_WIN), lambda i, r, t: (0, 0, 0)),
        ],
        out_specs=[
            pl.BlockSpec((1, n_phase, rows_per_step, A_WIDTH), lambda i, r, t: (i, r, t, 0)),
            pl.BlockSpec((1, n_phase, rows_per_step, A_WIDTH), lambda i, r, t: (i, r, t, 0)),
        ],
        out_shape=[
            jax.ShapeDtypeStruct((b, dil, ls, A_WIDTH), BF16),
            jax.ShapeDtypeStruct((b, dil, ls, A_WIDTH), F32),
        ],
        compiler_params=pltpu.CompilerParams(
            dimension_semantics=("arbitrary", "arbitrary", "arbitrary"),
            vmem_limit_bytes=VMEM_LIMIT_BYTES),
        name=f"band_attn_d{dil}",
    )(ph, ph, ph, jnp.asarray(_band_bias()))


B_TQ = 256
B_KB = 1024
B_TK = 256


def _diff_kernel(q_ref, k_ref, vt_ref, lam_ref, g_ref, o_ref, sa_ref, sb_ref, *, seq):
    lam_p = lam_ref[...]
    lam = (jnp.exp(jnp.sum(lam_p[0:1] * lam_p[1:2], axis=-1, keepdims=True))
           - jnp.exp(jnp.sum(lam_p[2:3] * lam_p[3:4], axis=-1, keepdims=True)) + LAMBDA_INIT)
    g = g_ref[...]

    n_kb = seq // B_KB
    n_qt = seq // B_TQ

    def load_qq(t):
        return _split_heads_rows(q_ref[0, pl.ds(pl.multiple_of(t * B_TQ, B_TQ), B_TQ), :])

    def scores(t, s_ref):
        qq = load_qq(t)
        cmax = None
        for kb in range(n_kb):
            s = lax.dot_general(k_ref[0, kb * B_KB:(kb + 1) * B_KB, :], qq, NT_DIMS,
                                preferred_element_type=F32)
            s_ref[kb * B_KB:(kb + 1) * B_KB, :] = s
            bmax = jnp.max(s, axis=0, keepdims=True)
            cmax = bmax if cmax is None else jnp.maximum(cmax, bmax)
        return cmax

    def softmax_pv(s_ref, m):
        l = jnp.zeros((1, 2 * B_TQ), F32)
        acc = jnp.zeros((2 * HEAD_DIM, 2 * B_TQ), F32)
        for c in range(seq // B_TK):
            p = jnp.exp2(s_ref[c * B_TK:(c + 1) * B_TK, :] - m)
            l = l + jnp.sum(p, axis=0, keepdims=True)
            acc = acc + jnp.dot(vt_ref[0, c], p.astype(BF16), preferred_element_type=F32)
        return l, acc

    def finish(t, l_acc):
        l, acc = l_acc
        o = acc * (1.0 / l)
        o = o[:, :B_TQ] - lam * o[:, B_TQ:]
        o = o * lax.rsqrt(jnp.mean(o * o, axis=0, keepdims=True) + SUBLN_EPS)
        q0 = pl.multiple_of(t * B_TQ, B_TQ)
        o_ref[0, pl.ds(q0, B_TQ), :] = ((o.T * g) * (1.0 - LAMBDA_INIT)).astype(BF16)

    def tile_pair(j, carry):
        m_a, pending = carry
        m_b = scores(2 * j + 1, sb_ref)
        finish(jnp.maximum(2 * j - 1, 0), pending)
        pending = softmax_pv(sa_ref, m_a)
        m_a = scores(2 * j + 2, sa_ref)
        finish(2 * j, pending)
        return m_a, softmax_pv(sb_ref, m_b)

    assert n_qt % 2 == 0 and n_qt >= 4
    placeholder = (jnp.ones((1, 2 * B_TQ), F32), jnp.zeros((2 * HEAD_DIM, 2 * B_TQ), F32))
    m_a, pending = lax.fori_loop(0, n_qt // 2 - 1, tile_pair, (scores(0, sa_ref), placeholder))
    m_b = scores(n_qt - 1, sb_ref)
    finish(n_qt - 3, pending)
    finish(n_qt - 2, softmax_pv(sa_ref, m_a))
    finish(n_qt - 1, softmax_pv(sb_ref, m_b))


def _diff_attention(qkb, vbt, lam_params, subln_g):
    b, s, _ = qkb.shape
    return pl.pallas_call(
        functools.partial(_diff_kernel, seq=s),
        grid=(b, B_HEADS),
        in_specs=[
            pl.BlockSpec((1, s, LANES), lambda i, h: (i, 0, h)),
            pl.BlockSpec((1, s, LANES), lambda i, h: (i, 0, B_HEADS + h)),
            pl.BlockSpec((1, s // B_TK, LANES, B_TK), lambda i, h: (i, 0, h, 0)),
            pl.BlockSpec((4, HEAD_DIM), lambda i, h: (0, 0)),
            pl.BlockSpec((1, 2 * HEAD_DIM), lambda i, h: (0, 0)),
        ],
        out_specs=pl.BlockSpec((1, s, LANES), lambda i, h: (i, 0, h)),
        out_shape=jax.ShapeDtypeStruct((b, s, B_WIDTH), BF16),
        scratch_shapes=[pltpu.VMEM((s, 2 * B_TQ), F32), pltpu.VMEM((s, 2 * B_TQ), F32)],
        compiler_params=pltpu.CompilerParams(
            dimension_semantics=("arbitrary", "arbitrary"), vmem_limit_bytes=VMEM_LIMIT_BYTES),
        name="diff_attn",
    )(qkb, qkb, vbt, lam_params, subln_g)


MERGE_TN = 512


def _split3_bf16(v):
    hi = v.astype(BF16)
    r1 = v - hi.astype(F32)
    mid = r1.astype(BF16)
    lo = (r1 - mid.astype(F32)).astype(BF16)
    return hi, mid, lo


def _merge_kernel(x_ref, g_ref, o1_ref, l1_ref, o4_ref, l4_ref, o16_ref, l16_ref, ob_ref,
                  p4_ref, p16_ref, wg_ref, wa_ref, wb_ref, wo_ref, out_ref):
    x = x_ref[0]
    tm, d = x.shape
    hb = _rms_norm(x, g_ref[...], NORM_EPS).astype(BF16)

    def token_order(o_ref_, l_ref_, p_ref_, dil):
        n = PERM // dil
        os_, ls_ = [], []
        for u in range(tm // PERM):
            xo = jnp.concatenate([o_ref_[0, r, u * n:(u + 1) * n, :] for r in range(dil)], axis=0)
            os_.append(jnp.dot(p_ref_[...], xo, preferred_element_type=F32))
            xl = jnp.concatenate([l_ref_[0, r, u * n:(u + 1) * n, :] for r in range(dil)], axis=0)
            ls_.append(sum(jnp.dot(p_ref_[...], part, preferred_element_type=F32)
                           for part in _split3_bf16(xl)))
        return jnp.concatenate(os_, axis=0), jnp.concatenate(ls_, axis=0)

    o4, l4 = token_order(o4_ref, l4_ref, p4_ref, 4)
    o16, l16 = token_order(o16_ref, l16_ref, p16_ref, 16)
    outs = [o1_ref[0, 0].astype(F32), o4, o16]
    lses = [l1_ref[0, 0], l4, l16]

    m = jnp.maximum(jnp.maximum(lses[0], lses[1]), lses[2])
    es = [jnp.exp2(v - m) for v in lses]
    num = es[0] * outs[0] + es[1] * outs[1] + es[2] * outs[2]
    attn_a = (num * (1.0 / (es[0] + es[1] + es[2]))).astype(BF16)
    attn_b = ob_ref[0]

    out = x
    for c in range(d // MERGE_TN):
        cs = slice(c * MERGE_TN, (c + 1) * MERGE_TN)
        ya = jnp.dot(attn_a, wa_ref[:, cs], preferred_element_type=F32)
        yb = jnp.dot(attn_b, wb_ref[:, cs], preferred_element_type=F32)
        ga = jnp.dot(hb, wg_ref[:, cs], preferred_element_type=F32)
        gb = jnp.dot(hb, wg_ref[:, d + c * MERGE_TN:d + (c + 1) * MERGE_TN], preferred_element_type=F32)
        merged = jax.nn.sigmoid(ga) * ya + jax.nn.sigmoid(gb) * yb
        out = out + jnp.dot(merged.astype(BF16), wo_ref[cs, :], preferred_element_type=F32)
    out_ref[0] = out


def _merge(x, g, o1, l1, o4, l4, o16, l16, ob, p4, p16, wg, wa, wb, wo):
    b, s, d = x.shape
    tm = TOKEN_TILE
    tok = lambda w: pl.BlockSpec((1, tm, w), lambda i, j: (i, j, 0))
    phase = lambda dil, w: pl.BlockSpec((1, dil, tm // dil, w), lambda i, j: (i, 0, j, 0))
    const = lambda a: pl.BlockSpec(a.shape, lambda i, j: (0,) * a.ndim)
    return pl.pallas_call(
        _merge_kernel,
        grid=(b, s // tm),
        in_specs=[tok(d), const(g), phase(1, A_WIDTH), phase(1, A_WIDTH), phase(4, A_WIDTH), phase(4, A_WIDTH),
                  phase(16, A_WIDTH), phase(16, A_WIDTH), tok(B_WIDTH), const(p4), const(p16),
                  const(wg), const(wa), const(wb), const(wo)],
        out_specs=tok(d),
        out_shape=jax.ShapeDtypeStruct((b, s, d), F32),
        compiler_params=pltpu.CompilerParams(
            dimension_semantics=("arbitrary", "arbitrary"), vmem_limit_bytes=VMEM_LIMIT_BYTES),
        name="merge_proj",
    )(x, g, o1, l1, o4, l4, o16, l16, ob, p4, p16, wg, wa, wb, wo)


MLP_TF = 512
MLP_TM = 1024


def _mlp_kernel(x_ref, g_ref, w1_ref, w2_ref, gf_ref, out_ref):
    x = x_ref[...]
    hb = _rms_norm(x, g_ref[...], NORM_EPS).astype(BF16)
    acc = x
    for c in range(w1_ref.shape[1] // MLP_TF):
        cs = slice(c * MLP_TF, (c + 1) * MLP_TF)
        u = jnp.maximum(jnp.dot(hb, w1_ref[:, cs], preferred_element_type=F32), 0.0)
        acc = acc + jnp.dot((u * u).astype(BF16), w2_ref[cs, :], preferred_element_type=F32)
    out_ref[...] = _rms_norm(acc, gf_ref[...], NORM_EPS)


def _mlp(x2d, g, w1, w2, gf):
    t, d = x2d.shape
    tm = MLP_TM
    row = pl.BlockSpec((tm, d), lambda i: (i, 0))
    full = lambda a: pl.BlockSpec(a.shape, lambda i: (0, 0), pipeline_mode=pl.Buffered(1))
    return pl.pallas_call(
        _mlp_kernel,
        grid=(t // tm,),
        in_specs=[row, full(g), full(w1), full(w2), full(gf)],
        out_specs=row,
        out_shape=jax.ShapeDtypeStruct((t, d), F32),
        compiler_params=pltpu.CompilerParams(
            dimension_semantics=("arbitrary",), vmem_limit_bytes=VMEM_LIMIT_BYTES),
        name="mlp_norm",
    )(x2d, g, w1, w2, gf)


def _rope_tables(seq):
    pos = jnp.arange(seq, dtype=F32)
    inv_freq = ROPE_THETA ** (-jnp.arange(0, HEAD_DIM, 2, dtype=F32) / HEAD_DIM)
    ang = pos[:, None] * inv_freq[None, :]
    cos = jnp.cos(ang)
    sin = jnp.sin(ang)
    cos = jnp.concatenate([cos, cos, cos, cos], axis=-1)
    sin = jnp.concatenate([-sin, sin, -sin, sin], axis=-1)
    return cos, sin


def kernel(x, w_in, w_branch_a, w_branch_b, w_out, lambda_q1, lambda_k1, lambda_q2, lambda_k2,
           diff_subln_g, norm_mix_g, norm_mlp_g, w_ff1, w_ff2, norm_final_g):
    b, s, d = x.shape
    assert w_in.shape[0] == 1, "single-layer block: the final norm is fused into the MLP kernel"
    assert s % B_KB == 0 and s % MLP_TM == 0 and s % QKV_TS == 0
    cos, sin = _rope_tables(s)
    w_in_l = w_in[0].astype(BF16)
    g_mix = norm_mix_g[0].reshape(1, d)
    vb_lo = QKV_A_WIDTH + QK_B_WIDTH
    p4 = jnp.asarray(_phase_perm(4), BF16)
    p16 = jnp.asarray(_phase_perm(16), BF16)
    qkva, qkb, vbt, ph4, ph16 = _qkv_proj(
        x, g_mix, w_in_l[:, :vb_lo], w_in_l[:, vb_lo:QKV_WIDTH].T, cos, sin, p4, p16)
    o1, l1 = _band_attention(qkva.reshape(b, 1, s, QKV_A_WIDTH), DILATED_PATTERNS[0][0])
    o4, l4 = _band_attention(ph4, DILATED_PATTERNS[1][0])
    o16, l16 = _band_attention(ph16, DILATED_PATTERNS[2][0])
    lam_params = jnp.stack([lambda_q1[0], lambda_k1[0], lambda_q2[0], lambda_k2[0]]).astype(F32)
    ob = _diff_attention(qkb, vbt, lam_params, diff_subln_g[0].reshape(1, 2 * HEAD_DIM).astype(F32))
    x1 = _merge(x, g_mix, o1, l1, o4, l4, o16, l16, ob, p4.T, p16.T, w_in_l[:, QKV_WIDTH:],
                w_branch_a[0].astype(BF16), w_branch_b[0].astype(BF16), w_out[0].astype(BF16))
    out = _mlp(x1.reshape(b * s, d), norm_mlp_g[0].reshape(1, d), w_ff1[0].astype(BF16),
               w_ff2[0].astype(BF16), norm_final_g.reshape(1, d))
    return out.reshape(b, s, d)
```

```python
import functools
import math

import numpy as np
import jax
import jax.numpy as jnp
from jax import lax
from jax.experimental import pallas as pl
from jax.experimental.pallas import tpu as pltpu

HEAD_DIM = 64
HALF_DIM = HEAD_DIM // 2
LANES = 128
A_HEADS = 8
A_WIDTH = A_HEADS * HEAD_DIM
B_HEADS = 4
B_WIDTH = B_HEADS * 2 * HEAD_DIM
QKV_A_WIDTH = 3 * A_WIDTH
QK_B_WIDTH = 2 * B_WIDTH
QKV_WIDTH = QKV_A_WIDTH + QK_B_WIDTH + B_WIDTH
DILATED_PATTERNS = ((128, 1), (512, 4), (2048, 16))
ROPE_THETA = 10000.0
NORM_EPS = 1e-6
SUBLN_EPS = 1e-5
LAMBDA_INIT = 0.8 - 0.6 * math.exp(-0.3 * 0)
QK_SCALE_LOG2E = math.log2(math.e) / math.sqrt(HEAD_DIM)
MASK_VALUE = -1e30
VMEM_LIMIT_BYTES = 56 * 1024 * 1024
TOKEN_TILE = 512
QKV_TS = 1024

BF16 = jnp.bfloat16
F32 = jnp.float32
NT_DIMS = (((1,), (1,)), ((), ()))


def _rms_norm(x, g, eps):
    return (x * lax.rsqrt(jnp.mean(x * x, axis=-1, keepdims=True) + eps)) * g


def _split_heads_rows(q):
    lane = lax.broadcasted_iota(jnp.int32, q.shape, 1)
    zero = jnp.zeros_like(q)
    return jnp.concatenate(
        [jnp.where(lane < HEAD_DIM, q, zero), jnp.where(lane >= HEAD_DIM, q, zero)], axis=0)


def _qkv_kernel(x_ref, g_ref, w_ref, wvt_ref, cos_ref, sin_ref,
                qkva_ref, qkb_ref, vbt_ref, ph4_ref, ph16_ref, slab_ref, slab4_ref):
    x = x_ref[0]
    ts = x.shape[0]
    hb = _rms_norm(x, g_ref[...], NORM_EPS).astype(BF16)
    cos = cos_ref[...]
    sin = sin_ref[...]
    lane = lax.broadcasted_iota(jnp.int32, cos.shape, 1)
    first_half = (lane % HEAD_DIM) < HALF_DIM

    def rope(tj, scale):
        rot = jnp.where(first_half, pltpu.roll(tj, LANES - HALF_DIM, 1), pltpu.roll(tj, HALF_DIM, 1))
        tj = tj * cos + rot * sin
        return tj * scale if scale != 1.0 else tj

    groups = A_WIDTH // LANES
    for c in range(5):
        t = jnp.dot(hb, w_ref[:, c * A_WIDTH:(c + 1) * A_WIDTH], preferred_element_type=F32)
        dst, base = (qkva_ref, c * A_WIDTH) if c < 3 else (qkb_ref, (c - 3) * A_WIDTH)
        scale = QK_SCALE_LOG2E if c in (0, 3) else 1.0
        for j in range(groups):
            tj = t[:, j * LANES:(j + 1) * LANES]
            if c != 2:
                tj = rope(tj, scale)
            dst[0, :, base + j * LANES:base + (j + 1) * LANES] = tj.astype(BF16)
            if c < 3:
                k = c * groups + j
                ks = slice(k * LANES, (k + 1) * LANES)
                slab, slab4 = slab_ref.at[k % 2], slab4_ref.at[k % 2]
                slab[...] = tj
                n4 = ts // 4
                for r4 in range(4):
                    ph = slab[pl.ds(r4, n4, stride=4), :]
                    ph4_ref[0, r4, :, ks] = ph.astype(BF16)
                    slab4[r4 * n4:(r4 + 1) * n4, :] = ph
                for r4 in range(4):
                    for a in range(4):
                        ph16_ref[0, 4 * a + r4, :, ks] = slab4[
                            pl.ds(r4 * n4 + a, ts // 16, stride=4), :].astype(BF16)
    vbt = lax.dot_general(wvt_ref[...], hb, NT_DIMS, preferred_element_type=F32).astype(BF16)
    for u in range(ts // B_TK):
        vbt_ref[0, u] = vbt[:, u * B_TK:(u + 1) * B_TK]


def _qkv_proj(x, g, w_qkv, w_vbt, cos, sin):
    b, s, d = x.shape
    ts = QKV_TS
    const = lambda a: pl.BlockSpec(a.shape, lambda i, j: (0,) * a.ndim, pipeline_mode=pl.Buffered(1))
    return pl.pallas_call(
        _qkv_kernel,
        grid=(b, s // ts),
        in_specs=[
            pl.BlockSpec((1, ts, d), lambda i, j: (i, j, 0)),
            const(g), const(w_qkv), const(w_vbt),
            pl.BlockSpec((ts, LANES), lambda i, j: (j, 0)),
            pl.BlockSpec((ts, LANES), lambda i, j: (j, 0)),
        ],
        out_specs=[
            pl.BlockSpec((1, ts, QKV_A_WIDTH), lambda i, j: (i, j, 0)),
            pl.BlockSpec((1, ts, QK_B_WIDTH), lambda i, j: (i, j, 0)),
            pl.BlockSpec((1, ts // B_TK, B_WIDTH, B_TK), lambda i, j: (i, j, 0, 0)),
            pl.BlockSpec((1, 4, ts // 4, QKV_A_WIDTH), lambda i, j: (i, 0, j, 0)),
            pl.BlockSpec((1, 16, ts // 16, QKV_A_WIDTH), lambda i, j: (i, 0, j, 0)),
        ],
        out_shape=[
            jax.ShapeDtypeStruct((b, s, QKV_A_WIDTH), BF16),
            jax.ShapeDtypeStruct((b, s, QK_B_WIDTH), BF16),
            jax.ShapeDtypeStruct((b, s // B_TK, B_WIDTH, B_TK), BF16),
            jax.ShapeDtypeStruct((b, 4, s // 4, QKV_A_WIDTH), BF16),
            jax.ShapeDtypeStruct((b, 16, s // 16, QKV_A_WIDTH), BF16),
        ],
        scratch_shapes=[pltpu.VMEM((2, ts, LANES), F32)] * 2,
        compiler_params=pltpu.CompilerParams(
            dimension_semantics=("arbitrary", "arbitrary"), vmem_limit_bytes=VMEM_LIMIT_BYTES),
        name="qkv_proj",
    )(x, g, w_qkv, w_vbt, cos, sin)


A_TQ = 128
A_WIN = 256
BAND_ROWS = 1024


def _band_bias():
    row = np.arange(2 * A_TQ)[:, None] % A_TQ
    col = np.arange(A_WIN)[None, :]
    half = (A_WIN - A_TQ) // 2
    return np.stack([np.where(np.abs(col - row + d) <= half, 0.0, MASK_VALUE)
                     for d in (-half, 0, -2 * half)]).astype(np.float32)


def _band_kernel(q_ref, k_ref, v_ref, bias_ref, o_ref, lse_ref, *, half, ls, rows_per_step):
    step = pl.program_id(2)
    n_sub = rows_per_step // A_TQ
    n_phase = q_ref.shape[1]
    out_lane = lax.broadcasted_iota(jnp.int32, (A_TQ, LANES), 1)

    def sub_tile(u, carry):
        ph = u // n_sub
        r0 = pl.multiple_of((u % n_sub) * A_TQ, A_TQ)
        t0 = step * rows_per_step + r0
        start = pl.multiple_of(jnp.clip(t0 - half, 0, ls - A_WIN), half)
        which = jnp.where(t0 == 0, 1, jnp.where(t0 == ls - A_TQ, 2, 0))
        for j in range(A_WIDTH // LANES):
            cs = slice(j * LANES, (j + 1) * LANES)
            qq = _split_heads_rows(q_ref[0, ph, pl.ds(r0, A_TQ), cs])
            kw = k_ref[0, ph, pl.ds(start, A_WIN), cs]
            vw = v_ref[0, ph, pl.ds(start, A_WIN), cs]
            s = lax.dot_general(qq, kw, NT_DIMS, preferred_element_type=F32) + bias_ref[which]
            m = jnp.max(s, axis=-1, keepdims=True)
            p = jnp.exp2(s - m)
            l = jnp.sum(p, axis=-1, keepdims=True)
            pv = jnp.dot(p.astype(BF16), vw, preferred_element_type=F32) * (1.0 / l)
            o_ref[0, ph, pl.ds(r0, A_TQ), cs] = jnp.where(
                out_lane < HEAD_DIM, pv[:A_TQ], pv[A_TQ:]).astype(BF16)
            lse = m + jnp.log2(l)
            lse_ref[0, ph, pl.ds(r0, A_TQ), cs] = jnp.where(out_lane < HEAD_DIM, lse[:A_TQ], lse[A_TQ:])
        return carry

    lax.fori_loop(0, n_phase * n_sub, sub_tile, 0, unroll=4)


def _band_attention(ph, window):
    b, dil, ls, _ = ph.shape
    half = window // (2 * dil)
    assert half == A_TQ // 2 and A_WIN == A_TQ + 2 * half and ls % A_TQ == 0 and ls >= A_WIN
    rows_per_step = min(ls, BAND_ROWS)
    n_phase = BAND_ROWS // rows_per_step
    assert dil % n_phase == 0
    kern = functools.partial(_band_kernel, half=half, ls=ls, rows_per_step=rows_per_step)
    return pl.pallas_call(
        kern,
        grid=(b, dil // n_phase, ls // rows_per_step),
        in_specs=[
            pl.BlockSpec((1, n_phase, rows_per_step, A_WIDTH), lambda i, r, t: (i, r, t, 0)),
            pl.BlockSpec((1, n_phase, ls, A_WIDTH), lambda i, r, t: (i, r, 0, 1)),
            pl.BlockSpec((1, n_phase, ls, A_WIDTH), lambda i, r, t: (i, r, 0, 2)),
            pl.BlockSpec((3, 2 * A_TQ, A_WIN), lambda i, r, t: (0, 0, 0)),
        ],
        out_specs=[
            pl.BlockSpec((1, n_phase, rows_per_step, A_WIDTH), lambda i, r, t: (i, r, t, 0)),
            pl.BlockSpec((1, n_phase, rows_per_step, A_WIDTH), lambda i, r, t: (i, r, t, 0)),
        ],
        out_shape=[
            jax.ShapeDtypeStruct((b, dil, ls, A_WIDTH), BF16),
            jax.ShapeDtypeStruct((b, dil, ls, A_WIDTH), F32),
        ],
        compiler_params=pltpu.CompilerParams(
            dimension_semantics=("arbitrary", "arbitrary", "arbitrary"),
            vmem_limit_bytes=VMEM_LIMIT_BYTES),
        name=f"band_attn_d{dil}",
    )(ph, ph, ph, jnp.asarray(_band_bias()))


B_TQ = 256
B_KB = 1024
B_TK = 256


def _diff_kernel(q_ref, k_ref, vt_ref, lam_ref, g_ref, o_ref, sa_ref, sb_ref, *, seq):
    lam_p = lam_ref[...]
    lam = (jnp.exp(jnp.sum(lam_p[0:1] * lam_p[1:2], axis=-1, keepdims=True))
           - jnp.exp(jnp.sum(lam_p[2:3] * lam_p[3:4], axis=-1, keepdims=True)) + LAMBDA_INIT)
    g = g_ref[...]

    n_kb = seq // B_KB
    n_qt = seq // B_TQ

    def load_qq(t):
        return _split_heads_rows(q_ref[0, pl.ds(pl.multiple_of(t * B_TQ, B_TQ), B_TQ), :])

    def scores(t, s_ref):
        qq = load_qq(t)
        cmax = None
        for kb in range(n_kb):
            s = lax.dot_general(k_ref[0, kb * B_KB:(kb + 1) * B_KB, :], qq, NT_DIMS,
                                preferred_element_type=F32)
            s_ref[kb * B_KB:(kb + 1) * B_KB, :] = s
            bmax = jnp.max(s, axis=0, keepdims=True)
            cmax = bmax if cmax is None else jnp.maximum(cmax, bmax)
        return cmax

    def softmax_pv(s_ref, m):
        l = jnp.zeros((1, 2 * B_TQ), F32)
        acc = jnp.zeros((2 * HEAD_DIM, 2 * B_TQ), F32)
        for c in range(seq // B_TK):
            p = jnp.exp2(s_ref[c * B_TK:(c + 1) * B_TK, :] - m)
            l = l + jnp.sum(p, axis=0, keepdims=True)
            acc = acc + jnp.dot(vt_ref[0, c], p.astype(BF16), preferred_element_type=F32)
        return l, acc

    def finish(t, l_acc):
        l, acc = l_acc
        o = acc * (1.0 / l)
        o = o[:, :B_TQ] - lam * o[:, B_TQ:]
        o = o * lax.rsqrt(jnp.mean(o * o, axis=0, keepdims=True) + SUBLN_EPS)
        q0 = pl.multiple_of(t * B_TQ, B_TQ)
        o_ref[0, pl.ds(q0, B_TQ), :] = ((o.T * g) * (1.0 - LAMBDA_INIT)).astype(BF16)

    def tile_pair(j, carry):
        m_a, pending = carry
        m_b = scores(2 * j + 1, sb_ref)
        finish(jnp.maximum(2 * j - 1, 0), pending)
        pending = softmax_pv(sa_ref, m_a)
        m_a = scores(2 * j + 2, sa_ref)
        finish(2 * j, pending)
        return m_a, softmax_pv(sb_ref, m_b)

    assert n_qt % 2 == 0 and n_qt >= 4
    placeholder = (jnp.ones((1, 2 * B_TQ), F32), jnp.zeros((2 * HEAD_DIM, 2 * B_TQ), F32))
    m_a, pending = lax.fori_loop(0, n_qt // 2 - 1, tile_pair, (scores(0, sa_ref), placeholder))
    m_b = scores(n_qt - 1, sb_ref)
    finish(n_qt - 3, pending)
    finish(n_qt - 2, softmax_pv(sa_ref, m_a))
    finish(n_qt - 1, softmax_pv(sb_ref, m_b))


def _diff_attention(qkb, vbt, lam_params, subln_g):
    b, s, _ = qkb.shape
    return pl.pallas_call(
        functools.partial(_diff_kernel, seq=s),
        grid=(b, B_HEADS),
        in_specs=[
            pl.BlockSpec((1, s, LANES), lambda i, h: (i, 0, h)),
            pl.BlockSpec((1, s, LANES), lambda i, h: (i, 0, B_HEADS + h)),
            pl.BlockSpec((1, s // B_TK, LANES, B_TK), lambda i, h: (i, 0, h, 0)),
            pl.BlockSpec((4, HEAD_DIM), lambda i, h: (0, 0)),
            pl.BlockSpec((1, 2 * HEAD_DIM), lambda i, h: (0, 0)),
        ],
        out_specs=pl.BlockSpec((1, s, LANES), lambda i, h: (i, 0, h)),
        out_shape=jax.ShapeDtypeStruct((b, s, B_WIDTH), BF16),
        scratch_shapes=[pltpu.VMEM((s, 2 * B_TQ), F32), pltpu.VMEM((s, 2 * B_TQ), F32)],
        compiler_params=pltpu.CompilerParams(
            dimension_semantics=("arbitrary", "arbitrary"), vmem_limit_bytes=VMEM_LIMIT_BYTES),
        name="diff_attn",
    )(qkb, qkb, vbt, lam_params, subln_g)


MERGE_TN = 512


def _merge_kernel(x_ref, g_ref, o1_ref, l1_ref, o4_ref, l4_ref, o16_ref, l16_ref, ob_ref,
                  wg_ref, wa_ref, wb_ref, wo_ref, out_ref, s0_ref, s1_ref, s2_ref, s3_ref):
    x = x_ref[0]
    tm, d = x.shape
    hb = _rms_norm(x, g_ref[...], NORM_EPS).astype(BF16)

    def token_order(src_ref, dil, scr_ref):
        n = tm // dil
        for r in range(dil):
            blk = src_ref[0, r].astype(F32)
            for j in range(A_WIDTH // LANES):
                scr_ref[j, pl.ds(r, n, stride=dil), :] = blk[:, j * LANES:(j + 1) * LANES]
        return jnp.concatenate([scr_ref[j] for j in range(A_WIDTH // LANES)], axis=-1)

    o4, l4 = token_order(o4_ref, 4, s0_ref), token_order(l4_ref, 4, s1_ref)
    o16, l16 = token_order(o16_ref, 16, s2_ref), token_order(l16_ref, 16, s3_ref)
    outs = [o1_ref[0, 0].astype(F32), o4, o16]
    lses = [l1_ref[0, 0], l4, l16]

    m = jnp.maximum(jnp.maximum(lses[0], lses[1]), lses[2])
    es = [jnp.exp2(v - m) for v in lses]
    num = es[0] * outs[0] + es[1] * outs[1] + es[2] * outs[2]
    attn_a = (num * (1.0 / (es[0] + es[1] + es[2]))).astype(BF16)
    attn_b = ob_ref[0]

    out = x
    for c in range(d // MERGE_TN):
        cs = slice(c * MERGE_TN, (c + 1) * MERGE_TN)
        ya = jnp.dot(attn_a, wa_ref[:, cs], preferred_element_type=F32)
        yb = jnp.dot(attn_b, wb_ref[:, cs], preferred_element_type=F32)
        ga = jnp.dot(hb, wg_ref[:, cs], preferred_element_type=F32)
        gb = jnp.dot(hb, wg_ref[:, d + c * MERGE_TN:d + (c + 1) * MERGE_TN], preferred_element_type=F32)
        merged = jax.nn.sigmoid(ga) * ya + jax.nn.sigmoid(gb) * yb
        out = out + jnp.dot(merged.astype(BF16), wo_ref[cs, :], preferred_element_type=F32)
    out_ref[0] = out


def _merge(x, g, o1, l1, o4, l4, o16, l16, ob, wg, wa, wb, wo):
    b, s, d = x.shape
    tm = TOKEN_TILE
    tok = lambda w: pl.BlockSpec((1, tm, w), lambda i, j: (i, j, 0))
    phase = lambda dil, w: pl.BlockSpec((1, dil, tm // dil, w), lambda i, j: (i, 0, j, 0))
    const = lambda a: pl.BlockSpec(a.shape, lambda i, j: (0,) * a.ndim)
    return pl.pallas_call(
        _merge_kernel,
        grid=(b, s // tm),
        in_specs=[tok(d), const(g), phase(1, A_WIDTH), phase(1, A_WIDTH), phase(4, A_WIDTH), phase(4, A_WIDTH),
                  phase(16, A_WIDTH), phase(16, A_WIDTH), tok(B_WIDTH),
                  const(wg), const(wa), const(wb), const(wo)],
        out_specs=tok(d),
        out_shape=jax.ShapeDtypeStruct((b, s, d), F32),
        scratch_shapes=[pltpu.VMEM((A_WIDTH // LANES, tm, LANES), F32)] * 4,
        compiler_params=pltpu.CompilerParams(
            dimension_semantics=("arbitrary", "arbitrary"), vmem_limit_bytes=VMEM_LIMIT_BYTES),
        name="merge_proj",
    )(x, g, o1, l1, o4, l4, o16, l16, ob, wg, wa, wb, wo)


MLP_TF = 512
MLP_TM = 1024


def _mlp_kernel(x_ref, g_ref, w1_ref, w2_ref, gf_ref, out_ref):
    x = x_ref[...]
    hb = _rms_norm(x, g_ref[...], NORM_EPS).astype(BF16)
    acc = x
    for c in range(w1_ref.shape[1] // MLP_TF):
        cs = slice(c * MLP_TF, (c + 1) * MLP_TF)
        u = jnp.maximum(jnp.dot(hb, w1_ref[:, cs], preferred_element_type=F32), 0.0)
        acc = acc + jnp.dot((u * u).astype(BF16), w2_ref[cs, :], preferred_element_type=F32)
    out_ref[...] = _rms_norm(acc, gf_ref[...], NORM_EPS)


def _mlp(x2d, g, w1, w2, gf):
    t, d = x2d.shape
    tm = MLP_TM
    row = pl.BlockSpec((tm, d), lambda i: (i, 0))
    full = lambda a: pl.BlockSpec(a.shape, lambda i: (0, 0), pipeline_mode=pl.Buffered(1))
    return pl.pallas_call(
        _mlp_kernel,
        grid=(t // tm,),
        in_specs=[row, full(g), full(w1), full(w2), full(gf)],
        out_specs=row,
        out_shape=jax.ShapeDtypeStruct((t, d), F32),
        compiler_params=pltpu.CompilerParams(
            dimension_semantics=("arbitrary",), vmem_limit_bytes=VMEM_LIMIT_BYTES),
        name="mlp_norm",
    )(x2d, g, w1, w2, gf)


def _rope_tables(seq):
    pos = jnp.arange(seq, dtype=F32)
    inv_freq = ROPE_THETA ** (-jnp.arange(0, HEAD_DIM, 2, dtype=F32) / HEAD_DIM)
    ang = pos[:, None] * inv_freq[None, :]
    cos = jnp.cos(ang)
    sin = jnp.sin(ang)
    cos = jnp.concatenate([cos, cos, cos, cos], axis=-1)
    sin = jnp.concatenate([-sin, sin, -sin, sin], axis=-1)
    return cos, sin


def kernel(x, w_in, w_branch_a, w_branch_b, w_out, lambda_q1, lambda_k1, lambda_q2, lambda_k2,
           diff_subln_g, norm_mix_g, norm_mlp_g, w_ff1, w_ff2, norm_final_g):
    b, s, d = x.shape
    assert w_in.shape[0] == 1, "single-layer block: the final norm is fused into the MLP kernel"
    assert s % B_KB == 0 and s % MLP_TM == 0 and s % QKV_TS == 0
    cos, sin = _rope_tables(s)
    w_in_l = w_in[0].astype(BF16)
    g_mix = norm_mix_g[0].reshape(1, d)
    vb_lo = QKV_A_WIDTH + QK_B_WIDTH
    qkva, qkb, vbt, ph4, ph16 = _qkv_proj(
        x, g_mix, w_in_l[:, :vb_lo], w_in_l[:, vb_lo:QKV_WIDTH].T, cos, sin)
    o1, l1 = _band_attention(qkva.reshape(b, 1, s, QKV_A_WIDTH), DILATED_PATTERNS[0][0])
    o4, l4 = _band_attention(ph4, DILATED_PATTERNS[1][0])
    o16, l16 = _band_attention(ph16, DILATED_PATTERNS[2][0])
    lam_params = jnp.stack([lambda_q1[0], lambda_k1[0], lambda_q2[0], lambda_k2[0]]).astype(F32)
    ob = _diff_attention(qkb, vbt, lam_params, diff_subln_g[0].reshape(1, 2 * HEAD_DIM).astype(F32))
    x1 = _merge(x, g_mix, o1, l1, o4, l4, o16, l16, ob, w_in_l[:, QKV_WIDTH:],
                w_branch_a[0].astype(BF16), w_branch_b[0].astype(BF16), w_out[0].astype(BF16))
    out = _mlp(x1.reshape(b * s, d), norm_mlp_g[0].reshape(1, d), w_ff1[0].astype(BF16),
               w_ff2[0].astype(BF16), norm_final_g.reshape(1, d))
    return out.reshape(b, s, d)
```

```python
import functools
import math

import numpy as np
import jax
import jax.numpy as jnp
from jax import lax
from jax.experimental import pallas as pl
from jax.experimental.pallas import tpu as pltpu

HEAD_DIM = 64
HALF_DIM = HEAD_DIM // 2
LANES = 128
A_HEADS = 8
A_WIDTH = A_HEADS * HEAD_DIM
B_HEADS = 4
B_WIDTH = B_HEADS * 2 * HEAD_DIM
QKV_A_WIDTH = 3 * A_WIDTH
QK_B_WIDTH = 2 * B_WIDTH
QKV_WIDTH = QKV_A_WIDTH + QK_B_WIDTH + B_WIDTH
DILATED_PATTERNS = ((128, 1), (512, 4), (2048, 16))
ROPE_THETA = 10000.0
NORM_EPS = 1e-6
SUBLN_EPS = 1e-5
LAMBDA_INIT = 0.8 - 0.6 * math.exp(-0.3 * 0)
QK_SCALE_LOG2E = math.log2(math.e) / math.sqrt(HEAD_DIM)
MASK_VALUE = -1e30
VMEM_LIMIT_BYTES = 56 * 1024 * 1024
PERM = 256
TOKEN_TILE = 512
QKV_TS = 1024

BF16 = jnp.bfloat16
F32 = jnp.float32
NT_DIMS = (((1,), (1,)), ((), ()))


def _rms_norm(x, g, eps):
    return (x * lax.rsqrt(jnp.mean(x * x, axis=-1, keepdims=True) + eps)) * g


def _split_heads_rows(q):
    lane = lax.broadcasted_iota(jnp.int32, q.shape, 1)
    zero = jnp.zeros_like(q)
    return jnp.concatenate(
        [jnp.where(lane < HEAD_DIM, q, zero), jnp.where(lane >= HEAD_DIM, q, zero)], axis=0)


def _phase_perm(dil):
    n = PERM // dil
    p = np.zeros((PERM, PERM), np.float32)
    for r in range(dil):
        for i in range(n):
            p[r * n + i, i * dil + r] = 1.0
    return p


def _qkv_kernel(x_ref, g_ref, w_ref, wvt_ref, cos_ref, sin_ref, p4_ref, p16_ref,
                qkva_ref, qkb_ref, vbt_ref, ph4_ref, ph16_ref):
    x = x_ref[0]
    ts = x.shape[0]
    hb = _rms_norm(x, g_ref[...], NORM_EPS).astype(BF16)
    cos = cos_ref[...]
    sin = sin_ref[...]
    lane = lax.broadcasted_iota(jnp.int32, cos.shape, 1)
    first_half = (lane % HEAD_DIM) < HALF_DIM

    def rope(t, scale):
        parts = []
        for j in range(t.shape[1] // LANES):
            tj = t[:, j * LANES:(j + 1) * LANES]
            rot = jnp.where(first_half, pltpu.roll(tj, LANES - HALF_DIM, 1), pltpu.roll(tj, HALF_DIM, 1))
            tj = tj * cos + rot * sin
            parts.append((tj * scale if scale != 1.0 else tj).astype(BF16))
        return parts

    for c in range(5):
        t = jnp.dot(hb, w_ref[:, c * A_WIDTH:(c + 1) * A_WIDTH], preferred_element_type=F32)
        dst, base = (qkva_ref, c * A_WIDTH) if c < 3 else (qkb_ref, (c - 3) * A_WIDTH)
        if c == 2:
            dst[0, :, base:base + A_WIDTH] = t.astype(BF16)
        else:
            scale = QK_SCALE_LOG2E if c in (0, 3) else 1.0
            for j, tj in enumerate(rope(t, scale)):
                dst[0, :, base + j * LANES:base + (j + 1) * LANES] = tj
    vbt = lax.dot_general(wvt_ref[...], hb, NT_DIMS, preferred_element_type=F32).astype(BF16)
    for u in range(ts // B_TK):
        vbt_ref[0, u] = vbt[:, u * B_TK:(u + 1) * B_TK]

    for dil, p_ref, ph_ref in ((4, p4_ref, ph4_ref), (16, p16_ref, ph16_ref)):
        n = PERM // dil
        for u in range(ts // PERM):
            for c in range(3):
                cs = slice(c * A_WIDTH, (c + 1) * A_WIDTH)
                tb = qkva_ref[0, u * PERM:(u + 1) * PERM, cs]
                ph = jnp.dot(p_ref[...], tb, preferred_element_type=F32).astype(BF16)
                for r in range(dil):
                    ph_ref[0, r, u * n:(u + 1) * n, cs] = ph[r * n:(r + 1) * n]


def _qkv_proj(x, g, w_qkv, w_vbt, cos, sin, p4, p16):
    b, s, d = x.shape
    ts = QKV_TS
    const = lambda a: pl.BlockSpec(a.shape, lambda i, j: (0,) * a.ndim, pipeline_mode=pl.Buffered(1))
    return pl.pallas_call(
        _qkv_kernel,
        grid=(b, s // ts),
        in_specs=[
            pl.BlockSpec((1, ts, d), lambda i, j: (i, j, 0)),
            const(g), const(w_qkv), const(w_vbt),
            pl.BlockSpec((ts, LANES), lambda i, j: (j, 0)),
            pl.BlockSpec((ts, LANES), lambda i, j: (j, 0)),
            const(p4), const(p16),
        ],
        out_specs=[
            pl.BlockSpec((1, ts, QKV_A_WIDTH), lambda i, j: (i, j, 0)),
            pl.BlockSpec((1, ts, QK_B_WIDTH), lambda i, j: (i, j, 0)),
            pl.BlockSpec((1, ts // B_TK, B_WIDTH, B_TK), lambda i, j: (i, j, 0, 0)),
            pl.BlockSpec((1, 4, ts // 4, QKV_A_WIDTH), lambda i, j: (i, 0, j, 0)),
            pl.BlockSpec((1, 16, ts // 16, QKV_A_WIDTH), lambda i, j: (i, 0, j, 0)),
        ],
        out_shape=[
            jax.ShapeDtypeStruct((b, s, QKV_A_WIDTH), BF16),
            jax.ShapeDtypeStruct((b, s, QK_B_WIDTH), BF16),
            jax.ShapeDtypeStruct((b, s // B_TK, B_WIDTH, B_TK), BF16),
            jax.ShapeDtypeStruct((b, 4, s // 4, QKV_A_WIDTH), BF16),
            jax.ShapeDtypeStruct((b, 16, s // 16, QKV_A_WIDTH), BF16),
        ],
        compiler_params=pltpu.CompilerParams(
            dimension_semantics=("arbitrary", "arbitrary"), vmem_limit_bytes=VMEM_LIMIT_BYTES),
        name="qkv_proj",
    )(x, g, w_qkv, w_vbt, cos, sin, p4, p16)


A_TQ = 128
A_WIN = 256
BAND_ROWS = 1024


def _band_bias():
    row = np.arange(2 * A_TQ)[:, None] % A_TQ
    col = np.arange(A_WIN)[None, :]
    half = (A_WIN - A_TQ) // 2
    return np.stack([np.where(np.abs(col - row + d) <= half, 0.0, MASK_VALUE)
                     for d in (-half, 0, -2 * half)]).astype(np.float32)


def _band_kernel(q_ref, k_ref, v_ref, bias_ref, o_ref, lse_ref, *, half, ls, rows_per_step):
    step = pl.program_id(2)
    n_sub = rows_per_step // A_TQ
    n_phase = q_ref.shape[1]
    out_lane = lax.broadcasted_iota(jnp.int32, (A_TQ, LANES), 1)

    def sub_tile(u, carry):
        ph = u // n_sub
        r0 = pl.multiple_of((u % n_sub) * A_TQ, A_TQ)
        t0 = step * rows_per_step + r0
        start = pl.multiple_of(jnp.clip(t0 - half, 0, ls - A_WIN), half)
        which = jnp.where(t0 == 0, 1, jnp.where(t0 == ls - A_TQ, 2, 0))
        for j in range(A_WIDTH // LANES):
            cs = slice(j * LANES, (j + 1) * LANES)
            qq = _split_heads_rows(q_ref[0, ph, pl.ds(r0, A_TQ), cs])
            kw = k_ref[0, ph, pl.ds(start, A_WIN), cs]
            vw = v_ref[0, ph, pl.ds(start, A_WIN), cs]
            s = lax.dot_general(qq, kw, NT_DIMS, preferred_element_type=F32) + bias_ref[which]
            m = jnp.max(s, axis=-1, keepdims=True)
            p = jnp.exp2(s - m)
            l = jnp.sum(p, axis=-1, keepdims=True)
            pv = jnp.dot(p.astype(BF16), vw, preferred_element_type=F32) * (1.0 / l)
            o_ref[0, ph, pl.ds(r0, A_TQ), cs] = jnp.where(
                out_lane < HEAD_DIM, pv[:A_TQ], pv[A_TQ:]).astype(BF16)
            lse = m + jnp.log2(l)
            lse_ref[0, ph, pl.ds(r0, A_TQ), cs] = jnp.where(out_lane < HEAD_DIM, lse[:A_TQ], lse[A_TQ:])
        return carry

    lax.fori_loop(0, n_phase * n_sub, sub_tile, 0, unroll=8)


def _band_attention(ph, window):
    b, dil, ls, _ = ph.shape
    half = window // (2 * dil)
    assert half == A_TQ // 2 and A_WIN == A_TQ + 2 * half and ls % A_TQ == 0 and ls >= A_WIN
    rows_per_step = min(ls, BAND_ROWS)
    n_phase = BAND_ROWS // rows_per_step
    assert dil % n_phase == 0
    kern = functools.partial(_band_kernel, half=half, ls=ls, rows_per_step=rows_per_step)
    return pl.pallas_call(
        kern,
        grid=(b, dil // n_phase, ls // rows_per_step),
        in_specs=[
            pl.BlockSpec((1, n_phase, rows_per_step, A_WIDTH), lambda i, r, t: (i, r, t, 0)),
            pl.BlockSpec((1, n_phase, ls, A_WIDTH), lambda i, r, t: (i, r, 0, 1)),
            pl.BlockSpec((1, n_phase, ls, A_WIDTH), lambda i, r, t: (i, r, 0, 2)),
            pl.BlockSpec((3, 2 * A_TQ, A_WIN), lambda i, r, t: (0, 0, 0)),
        ],
        out_specs=[
            pl.BlockSpec((1, n_phase, rows_per_step, A_WIDTH), lambda i, r, t: (i, r, t, 0)),
            pl.BlockSpec((1, n_phase, rows_per_step, A_WIDTH), lambda i, r, t: (i, r, t, 0)),
        ],
        out_shape=[
            jax.ShapeDtypeStruct((b, dil, ls, A_WIDTH), BF16),
            jax.ShapeDtypeStruct((b, dil, ls, A_WIDTH), F32),
        ],
        compiler_params=pltpu.CompilerParams(
            dimension_semantics=("arbitrary", "arbitrary", "arbitrary"),
            vmem_limit_bytes=VMEM_LIMIT_BYTES),
        name=f"band_attn_d{dil}",
    )(ph, ph, ph, jnp.asarray(_band_bias()))


B_TQ = 256
B_KB = 1024
B_TK = 256


def _diff_kernel(q_ref, k_ref, vt_ref, lam_ref, g_ref, o_ref, sa_ref, sb_ref, *, seq):
    lam_p = lam_ref[...]
    lam = (jnp.exp(jnp.sum(lam_p[0:1] * lam_p[1:2], axis=-1, keepdims=True))
           - jnp.exp(jnp.sum(lam_p[2:3] * lam_p[3:4], axis=-1, keepdims=True)) + LAMBDA_INIT)
    g = g_ref[...]

    n_kb = seq // B_KB
    n_qt = seq // B_TQ

    def load_qq(t):
        return _split_heads_rows(q_ref[0, pl.ds(pl.multiple_of(t * B_TQ, B_TQ), B_TQ), :])

    def scores(t, s_ref):
        qq = load_qq(t)
        cmax = None
        for kb in range(n_kb):
            s = lax.dot_general(k_ref[0, kb * B_KB:(kb + 1) * B_KB, :], qq, NT_DIMS,
                                preferred_element_type=F32)
            s_ref[kb * B_KB:(kb + 1) * B_KB, :] = s
            bmax = jnp.max(s, axis=0, keepdims=True)
            cmax = bmax if cmax is None else jnp.maximum(cmax, bmax)
        return cmax

    def softmax_pv(s_ref, m):
        l = jnp.zeros((1, 2 * B_TQ), F32)
        acc = jnp.zeros((2 * HEAD_DIM, 2 * B_TQ), F32)
        for c in range(seq // B_TK):
            p = jnp.exp2(s_ref[c * B_TK:(c + 1) * B_TK, :] - m)
            l = l + jnp.sum(p, axis=0, keepdims=True)
            acc = acc + jnp.dot(vt_ref[0, c], p.astype(BF16), preferred_element_type=F32)
        return l, acc

    def finish(t, l_acc):
        l, acc = l_acc
        o = acc * (1.0 / l)
        o = o[:, :B_TQ] - lam * o[:, B_TQ:]
        o = o * lax.rsqrt(jnp.mean(o * o, axis=0, keepdims=True) + SUBLN_EPS)
        q0 = pl.multiple_of(t * B_TQ, B_TQ)
        o_ref[0, pl.ds(q0, B_TQ), :] = ((o.T * g) * (1.0 - LAMBDA_INIT)).astype(BF16)

    def tile_pair(j, carry):
        m_a, pending = carry
        m_b = scores(2 * j + 1, sb_ref)
        finish(jnp.maximum(2 * j - 1, 0), pending)
        pending = softmax_pv(sa_ref, m_a)
        m_a = scores(2 * j + 2, sa_ref)
        finish(2 * j, pending)
        return m_a, softmax_pv(sb_ref, m_b)

    assert n_qt % 2 == 0 and n_qt >= 4
    placeholder = (jnp.ones((1, 2 * B_TQ), F32), jnp.zeros((2 * HEAD_DIM, 2 * B_TQ), F32))
    m_a, pending = lax.fori_loop(0, n_qt // 2 - 1, tile_pair, (scores(0, sa_ref), placeholder))
    m_b = scores(n_qt - 1, sb_ref)
    finish(n_qt - 3, pending)
    finish(n_qt - 2, softmax_pv(sa_ref, m_a))
    finish(n_qt - 1, softmax_pv(sb_ref, m_b))


def _diff_attention(qkb, vbt, lam_params, subln_g):
    b, s, _ = qkb.shape
    return pl.pallas_call(
        functools.partial(_diff_kernel, seq=s),
        grid=(b, B_HEADS),
        in_specs=[
            pl.BlockSpec((1, s, LANES), lambda i, h: (i, 0, h)),
            pl.BlockSpec((1, s, LANES), lambda i, h: (i, 0, B_HEADS + h)),
            pl.BlockSpec((1, s // B_TK, LANES, B_TK), lambda i, h: (i, 0, h, 0)),
            pl.BlockSpec((4, HEAD_DIM), lambda i, h: (0, 0)),
            pl.BlockSpec((1, 2 * HEAD_DIM), lambda i, h: (0, 0)),
        ],
        out_specs=pl.BlockSpec((1, s, LANES), lambda i, h: (i, 0, h)),
        out_shape=jax.ShapeDtypeStruct((b, s, B_WIDTH), BF16),
        scratch_shapes=[pltpu.VMEM((s, 2 * B_TQ), F32), pltpu.VMEM((s, 2 * B_TQ), F32)],
        compiler_params=pltpu.CompilerParams(
            dimension_semantics=("arbitrary", "arbitrary"), vmem_limit_bytes=VMEM_LIMIT_BYTES),
        name="diff_attn",
    )(qkb, qkb, vbt, lam_params, subln_g)


MERGE_TN = 1024


def _merge_kernel(x_ref, g_ref, o1_ref, l1_ref, o4_ref, l4_ref, o16_ref, l16_ref, ob_ref,
                  wg_ref, wa_ref, wb_ref, wo_ref, out_ref, s0_ref, s1_ref, s2_ref, s3_ref):
    x = x_ref[0]
    tm, d = x.shape
    hb = _rms_norm(x, g_ref[...], NORM_EPS).astype(BF16)

    def token_order(src_ref, dil, scr_ref):
        n = tm // dil
        for r in range(dil):
            blk = src_ref[0, r].astype(F32)
            for j in range(A_WIDTH // LANES):
                scr_ref[j, pl.ds(r, n, stride=dil), :] = blk[:, j * LANES:(j + 1) * LANES]
        return jnp.concatenate([scr_ref[j] for j in range(A_WIDTH // LANES)], axis=-1)

    o4, l4 = token_order(o4_ref, 4, s0_ref), token_order(l4_ref, 4, s1_ref)
    o16, l16 = token_order(o16_ref, 16, s2_ref), token_order(l16_ref, 16, s3_ref)
    outs = [o1_ref[0, 0].astype(F32), o4, o16]
    lses = [l1_ref[0, 0], l4, l16]

    m = jnp.maximum(jnp.maximum(lses[0], lses[1]), lses[2])
    es = [jnp.exp2(v - m) for v in lses]
    num = es[0] * outs[0] + es[1] * outs[1] + es[2] * outs[2]
    attn_a = (num * (1.0 / (es[0] + es[1] + es[2]))).astype(BF16)
    attn_b = ob_ref[0]

    out = x
    for c in range(d // MERGE_TN):
        cs = slice(c * MERGE_TN, (c + 1) * MERGE_TN)
        ya = jnp.dot(attn_a, wa_ref[:, cs], preferred_element_type=F32)
        yb = jnp.dot(attn_b, wb_ref[:, cs], preferred_element_type=F32)
        ga = jnp.dot(hb, wg_ref[:, cs], preferred_element_type=F32)
        gb = jnp.dot(hb, wg_ref[:, d + c * MERGE_TN:d + (c + 1) * MERGE_TN], preferred_element_type=F32)
        merged = jax.nn.sigmoid(ga) * ya + jax.nn.sigmoid(gb) * yb
        out = out + jnp.dot(merged.astype(BF16), wo_ref[cs, :], preferred_element_type=F32)
    out_ref[0] = out


def _merge(x, g, o1, l1, o4, l4, o16, l16, ob, wg, wa, wb, wo):
    b, s, d = x.shape
    tm = TOKEN_TILE
    tok = lambda w: pl.BlockSpec((1, tm, w), lambda i, j: (i, j, 0))
    phase = lambda dil, w: pl.BlockSpec((1, dil, tm // dil, w), lambda i, j: (i, 0, j, 0))
    const = lambda a: pl.BlockSpec(a.shape, lambda i, j: (0,) * a.ndim)
    return pl.pallas_call(
        _merge_kernel,
        grid=(b, s // tm),
        in_specs=[tok(d), const(g), phase(1, A_WIDTH), phase(1, A_WIDTH), phase(4, A_WIDTH), phase(4, A_WIDTH),
                  phase(16, A_WIDTH), phase(16, A_WIDTH), tok(B_WIDTH),
                  const(wg), const(wa), const(wb), const(wo)],
        out_specs=tok(d),
        out_shape=jax.ShapeDtypeStruct((b, s, d), F32),
        scratch_shapes=[pltpu.VMEM((A_WIDTH // LANES, tm, LANES), F32)] * 4,
        compiler_params=pltpu.CompilerParams(
            dimension_semantics=("arbitrary", "arbitrary"), vmem_limit_bytes=VMEM_LIMIT_BYTES),
        name="merge_proj",
    )(x, g, o1, l1, o4, l4, o16, l16, ob, wg, wa, wb, wo)


MLP_TF = 512
MLP_TM = 1024


def _mlp_kernel(x_ref, g_ref, w1_ref, w2_ref, gf_ref, out_ref):
    x = x_ref[...]
    hb = _rms_norm(x, g_ref[...], NORM_EPS).astype(BF16)
    acc = x
    for c in range(w1_ref.shape[1] // MLP_TF):
        cs = slice(c * MLP_TF, (c + 1) * MLP_TF)
        u = jnp.maximum(jnp.dot(hb, w1_ref[:, cs], preferred_element_type=F32), 0.0)
        acc = acc + jnp.dot((u * u).astype(BF16), w2_ref[cs, :], preferred_element_type=F32)
    out_ref[...] = _rms_norm(acc, gf_ref[...], NORM_EPS)


def _mlp(x2d, g, w1, w2, gf):
    t, d = x2d.shape
    tm = MLP_TM
    row = pl.BlockSpec((tm, d), lambda i: (i, 0))
    full = lambda a: pl.BlockSpec(a.shape, lambda i: (0, 0), pipeline_mode=pl.Buffered(1))
    return pl.pallas_call(
        _mlp_kernel,
        grid=(t // tm,),
        in_specs=[row, full(g), full(w1), full(w2), full(gf)],
        out_specs=row,
        out_shape=jax.ShapeDtypeStruct((t, d), F32),
        compiler_params=pltpu.CompilerParams(
            dimension_semantics=("arbitrary",), vmem_limit_bytes=VMEM_LIMIT_BYTES),
        name="mlp_norm",
    )(x2d, g, w1, w2, gf)


def _rope_tables(seq):
    pos = jnp.arange(seq, dtype=F32)
    inv_freq = ROPE_THETA ** (-jnp.arange(0, HEAD_DIM, 2, dtype=F32) / HEAD_DIM)
    ang = pos[:, None] * inv_freq[None, :]
    cos = jnp.cos(ang)
    sin = jnp.sin(ang)
    cos = jnp.concatenate([cos, cos, cos, cos], axis=-1)
    sin = jnp.concatenate([-sin, sin, -sin, sin], axis=-1)
    return cos, sin


def kernel(x, w_in, w_branch_a, w_branch_b, w_out, lambda_q1, lambda_k1, lambda_q2, lambda_k2,
           diff_subln_g, norm_mix_g, norm_mlp_g, w_ff1, w_ff2, norm_final_g):
    b, s, d = x.shape
    assert w_in.shape[0] == 1, "single-layer block: the final norm is fused into the MLP kernel"
    assert s % B_KB == 0 and s % MLP_TM == 0 and s % QKV_TS == 0
    cos, sin = _rope_tables(s)
    w_in_l = w_in[0].astype(BF16)
    g_mix = norm_mix_g[0].reshape(1, d)
    vb_lo = QKV_A_WIDTH + QK_B_WIDTH
    qkva, qkb, vbt, ph4, ph16 = _qkv_proj(
        x, g_mix, w_in_l[:, :vb_lo], w_in_l[:, vb_lo:QKV_WIDTH].T, cos, sin,
        jnp.asarray(_phase_perm(4), BF16), jnp.asarray(_phase_perm(16), BF16))
    o1, l1 = _band_attention(qkva.reshape(b, 1, s, QKV_A_WIDTH), DILATED_PATTERNS[0][0])
    o4, l4 = _band_attention(ph4, DILATED_PATTERNS[1][0])
    o16, l16 = _band_attention(ph16, DILATED_PATTERNS[2][0])
    lam_params = jnp.stack([lambda_q1[0], lambda_k1[0], lambda_q2[0], lambda_k2[0]]).astype(F32)
    ob = _diff_attention(qkb, vbt, lam_params, diff_subln_g[0].reshape(1, 2 * HEAD_DIM).astype(F32))
    x1 = _merge(x, g_mix, o1, l1, o4, l4, o16, l16, ob, w_in_l[:, QKV_WIDTH:],
                w_branch_a[0].astype(BF16), w_branch_b[0].astype(BF16), w_out[0].astype(BF16))
    out = _mlp(x1.reshape(b * s, d), norm_mlp_g[0].reshape(1, d), w_ff1[0].astype(BF16),
               w_ff2[0].astype(BF16), norm_final_g.reshape(1, d))
    return out.reshape(b, s, d)
```

```python
import functools
import math

import numpy as np
import jax
import jax.numpy as jnp
from jax import lax
from jax.experimental import pallas as pl
from jax.experimental.pallas import tpu as pltpu

HEAD_DIM = 64
HALF_DIM = HEAD_DIM // 2
LANES = 128
A_HEADS = 8
A_WIDTH = A_HEADS * HEAD_DIM
B_HEADS = 4
B_WIDTH = B_HEADS * 2 * HEAD_DIM
QKV_A_WIDTH = 3 * A_WIDTH
QK_B_WIDTH = 2 * B_WIDTH
QKV_WIDTH = QKV_A_WIDTH + QK_B_WIDTH + B_WIDTH
DILATED_PATTERNS = ((128, 1), (512, 4), (2048, 16))
ROPE_THETA = 10000.0
NORM_EPS = 1e-6
SUBLN_EPS = 1e-5
LAMBDA_INIT = 0.8 - 0.6 * math.exp(-0.3 * 0)
QK_SCALE_LOG2E = math.log2(math.e) / math.sqrt(HEAD_DIM)
MASK_VALUE = -1e30
VMEM_LIMIT_BYTES = 56 * 1024 * 1024
PERM = 256
TOKEN_TILE = 512
QKV_TS = 1024

BF16 = jnp.bfloat16
F32 = jnp.float32
NT_DIMS = (((1,), (1,)), ((), ()))


def _rms_norm(x, g, eps):
    return (x * lax.rsqrt(jnp.mean(x * x, axis=-1, keepdims=True) + eps)) * g


def _split_heads_rows(q):
    lane = lax.broadcasted_iota(jnp.int32, q.shape, 1)
    zero = jnp.zeros_like(q)
    return jnp.concatenate(
        [jnp.where(lane < HEAD_DIM, q, zero), jnp.where(lane >= HEAD_DIM, q, zero)], axis=0)


def _phase_perm(dil):
    n = PERM // dil
    p = np.zeros((PERM, PERM), np.float32)
    for r in range(dil):
        for i in range(n):
            p[r * n + i, i * dil + r] = 1.0
    return p


def _qkv_kernel(x_ref, g_ref, w_ref, wvt_ref, cos_ref, sin_ref, p4_ref, p16_ref,
                qkva_ref, qkb_ref, vbt_ref, ph4_ref, ph16_ref):
    x = x_ref[0]
    ts = x.shape[0]
    hb = _rms_norm(x, g_ref[...], NORM_EPS).astype(BF16)
    cos = cos_ref[...]
    sin = sin_ref[...]
    lane = lax.broadcasted_iota(jnp.int32, cos.shape, 1)
    first_half = (lane % HEAD_DIM) < HALF_DIM

    def rope(t, scale):
        parts = []
        for j in range(t.shape[1] // LANES):
            tj = t[:, j * LANES:(j + 1) * LANES]
            rot = jnp.where(first_half, pltpu.roll(tj, LANES - HALF_DIM, 1), pltpu.roll(tj, HALF_DIM, 1))
            tj = tj * cos + rot * sin
            parts.append((tj * scale if scale != 1.0 else tj).astype(BF16))
        return parts

    for c in range(5):
        t = jnp.dot(hb, w_ref[:, c * A_WIDTH:(c + 1) * A_WIDTH], preferred_element_type=F32)
        dst, base = (qkva_ref, c * A_WIDTH) if c < 3 else (qkb_ref, (c - 3) * A_WIDTH)
        if c == 2:
            dst[0, :, base:base + A_WIDTH] = t.astype(BF16)
        else:
            scale = QK_SCALE_LOG2E if c in (0, 3) else 1.0
            for j, tj in enumerate(rope(t, scale)):
                dst[0, :, base + j * LANES:base + (j + 1) * LANES] = tj
    vbt = lax.dot_general(wvt_ref[...], hb, NT_DIMS, preferred_element_type=F32).astype(BF16)
    for u in range(ts // B_TK):
        vbt_ref[0, u] = vbt[:, u * B_TK:(u + 1) * B_TK]

    for dil, p_ref, ph_ref in ((4, p4_ref, ph4_ref), (16, p16_ref, ph16_ref)):
        n = PERM // dil
        for u in range(ts // PERM):
            for c in range(3):
                cs = slice(c * A_WIDTH, (c + 1) * A_WIDTH)
                tb = qkva_ref[0, u * PERM:(u + 1) * PERM, cs]
                ph = jnp.dot(p_ref[...], tb, preferred_element_type=F32).astype(BF16)
                for r in range(dil):
                    ph_ref[0, r, u * n:(u + 1) * n, cs] = ph[r * n:(r + 1) * n]


def _qkv_proj(x, g, w_qkv, w_vbt, cos, sin, p4, p16):
    b, s, d = x.shape
    ts = QKV_TS
    const = lambda a: pl.BlockSpec(a.shape, lambda i, j: (0,) * a.ndim, pipeline_mode=pl.Buffered(1))
    return pl.pallas_call(
        _qkv_kernel,
        grid=(b, s // ts),
        in_specs=[
            pl.BlockSpec((1, ts, d), lambda i, j: (i, j, 0)),
            const(g), const(w_qkv), const(w_vbt),
            pl.BlockSpec((ts, LANES), lambda i, j: (j, 0)),
            pl.BlockSpec((ts, LANES), lambda i, j: (j, 0)),
            const(p4), const(p16),
        ],
        out_specs=[
            pl.BlockSpec((1, ts, QKV_A_WIDTH), lambda i, j: (i, j, 0)),
            pl.BlockSpec((1, ts, QK_B_WIDTH), lambda i, j: (i, j, 0)),
            pl.BlockSpec((1, ts // B_TK, B_WIDTH, B_TK), lambda i, j: (i, j, 0, 0)),
            pl.BlockSpec((1, 4, ts // 4, QKV_A_WIDTH), lambda i, j: (i, 0, j, 0)),
            pl.BlockSpec((1, 16, ts // 16, QKV_A_WIDTH), lambda i, j: (i, 0, j, 0)),
        ],
        out_shape=[
            jax.ShapeDtypeStruct((b, s, QKV_A_WIDTH), BF16),
            jax.ShapeDtypeStruct((b, s, QK_B_WIDTH), BF16),
            jax.ShapeDtypeStruct((b, s // B_TK, B_WIDTH, B_TK), BF16),
            jax.ShapeDtypeStruct((b, 4, s // 4, QKV_A_WIDTH), BF16),
            jax.ShapeDtypeStruct((b, 16, s // 16, QKV_A_WIDTH), BF16),
        ],
        compiler_params=pltpu.CompilerParams(
            dimension_semantics=("arbitrary", "arbitrary"), vmem_limit_bytes=VMEM_LIMIT_BYTES),
        name="qkv_proj",
    )(x, g, w_qkv, w_vbt, cos, sin, p4, p16)


A_TQ = 128
A_WIN = 256
BAND_ROWS = 2048


def _band_bias():
    row = np.arange(2 * A_TQ)[:, None] % A_TQ
    col = np.arange(A_WIN)[None, :]
    half = (A_WIN - A_TQ) // 2
    return np.stack([np.where(np.abs(col - row + d) <= half, 0.0, MASK_VALUE)
                     for d in (-half, 0, -2 * half)]).astype(np.float32)


def _band_kernel(q_ref, k_ref, v_ref, bias_ref, o_ref, lse_ref, *, half, ls, rows_per_step):
    step = pl.program_id(2)
    n_sub = rows_per_step // A_TQ
    n_phase = q_ref.shape[1]
    out_lane = lax.broadcasted_iota(jnp.int32, (A_TQ, LANES), 1)

    def sub_tile(u, carry):
        ph = u // n_sub
        r0 = pl.multiple_of((u % n_sub) * A_TQ, A_TQ)
        t0 = step * rows_per_step + r0
        start = pl.multiple_of(jnp.clip(t0 - half, 0, ls - A_WIN), half)
        which = jnp.where(t0 == 0, 1, jnp.where(t0 == ls - A_TQ, 2, 0))
        for j in range(A_WIDTH // LANES):
            cs = slice(j * LANES, (j + 1) * LANES)
            qq = _split_heads_rows(q_ref[0, ph, pl.ds(r0, A_TQ), cs])
            kw = k_ref[0, ph, pl.ds(start, A_WIN), cs]
            vw = v_ref[0, ph, pl.ds(start, A_WIN), cs]
            s = lax.dot_general(qq, kw, NT_DIMS, preferred_element_type=F32) + bias_ref[which]
            m = jnp.max(s, axis=-1, keepdims=True)
            p = jnp.exp2(s - m)
            l = jnp.sum(p, axis=-1, keepdims=True)
            pv = jnp.dot(p.astype(BF16), vw, preferred_element_type=F32) * (1.0 / l)
            o_ref[0, ph, pl.ds(r0, A_TQ), cs] = jnp.where(
                out_lane < HEAD_DIM, pv[:A_TQ], pv[A_TQ:]).astype(BF16)
            lse = m + jnp.log2(l)
            lse_ref[0, ph, pl.ds(r0, A_TQ), cs] = jnp.where(out_lane < HEAD_DIM, lse[:A_TQ], lse[A_TQ:])
        return carry

    lax.fori_loop(0, n_phase * n_sub, sub_tile, 0, unroll=8)


def _band_attention(ph, window):
    b, dil, ls, _ = ph.shape
    half = window // (2 * dil)
    assert half == A_TQ // 2 and A_WIN == A_TQ + 2 * half and ls % A_TQ == 0 and ls >= A_WIN
    rows_per_step = min(ls, BAND_ROWS)
    n_phase = BAND_ROWS // rows_per_step
    assert dil % n_phase == 0
    kern = functools.partial(_band_kernel, half=half, ls=ls, rows_per_step=rows_per_step)
    return pl.pallas_call(
        kern,
        grid=(b, dil // n_phase, ls // rows_per_step),
        in_specs=[
            pl.BlockSpec((1, n_phase, rows_per_step, A_WIDTH), lambda i, r, t: (i, r, t, 0)),
            pl.BlockSpec((1, n_phase, ls, A_WIDTH), lambda i, r, t: (i, r, 0, 1)),
            pl.BlockSpec((1, n_phase, ls, A_WIDTH), lambda i, r, t: (i, r, 0, 2)),
            pl.BlockSpec((3, 2 * A_TQ, A_WIN), lambda i, r, t: (0, 0, 0)),
        ],
        out_specs=[
            pl.BlockSpec((1, n_phase, rows_per_step, A_WIDTH), lambda i, r, t: (i, r, t, 0)),
            pl.BlockSpec((1, n_phase, rows_per_step, A_WIDTH), lambda i, r, t: (i, r, t, 0)),
        ],
        out_shape=[
            jax.ShapeDtypeStruct((b, dil, ls, A_WIDTH), BF16),
            jax.ShapeDtypeStruct((b, dil, ls, A_WIDTH), F32),
        ],
        compiler_params=pltpu.CompilerParams(
            dimension_semantics=("arbitrary", "arbitrary", "arbitrary"),
            vmem_limit_bytes=VMEM_LIMIT_BYTES),
        name=f"band_attn_d{dil}",
    )(ph, ph, ph, jnp.asarray(_band_bias()))


B_TQ = 256
B_KB = 1024
B_TK = 256


def _diff_kernel(q_ref, k_ref, vt_ref, lam_ref, g_ref, o_ref, sa_ref, sb_ref, *, seq):
    lam_p = lam_ref[...]
    lam = (jnp.exp(jnp.sum(lam_p[0:1] * lam_p[1:2], axis=-1, keepdims=True))
           - jnp.exp(jnp.sum(lam_p[2:3] * lam_p[3:4], axis=-1, keepdims=True)) + LAMBDA_INIT)
    g = g_ref[...]

    n_kb = seq // B_KB
    n_qt = seq // B_TQ

    def load_qq(t):
        return _split_heads_rows(q_ref[0, pl.ds(pl.multiple_of(t * B_TQ, B_TQ), B_TQ), :])

    def scores(t, s_ref):
        qq = load_qq(t)
        cmax = None
        for kb in range(n_kb):
            s = lax.dot_general(k_ref[0, kb * B_KB:(kb + 1) * B_KB, :], qq, NT_DIMS,
                                preferred_element_type=F32)
            s_ref[kb * B_KB:(kb + 1) * B_KB, :] = s
            bmax = jnp.max(s, axis=0, keepdims=True)
            cmax = bmax if cmax is None else jnp.maximum(cmax, bmax)
        return cmax

    def softmax_pv(s_ref, m):
        l = jnp.zeros((1, 2 * B_TQ), F32)
        acc = jnp.zeros((2 * HEAD_DIM, 2 * B_TQ), F32)
        for c in range(seq // B_TK):
            p = jnp.exp2(s_ref[c * B_TK:(c + 1) * B_TK, :] - m)
            l = l + jnp.sum(p, axis=0, keepdims=True)
            acc = acc + jnp.dot(vt_ref[0, c], p.astype(BF16), preferred_element_type=F32)
        return l, acc

    def finish(t, l_acc):
        l, acc = l_acc
        o = acc * (1.0 / l)
        o = o[:, :B_TQ] - lam * o[:, B_TQ:]
        o = o * lax.rsqrt(jnp.mean(o * o, axis=0, keepdims=True) + SUBLN_EPS)
        q0 = pl.multiple_of(t * B_TQ, B_TQ)
        o_ref[0, pl.ds(q0, B_TQ), :] = ((o.T * g) * (1.0 - LAMBDA_INIT)).astype(BF16)

    def tile_pair(j, carry):
        m_a, pending = carry
        m_b = scores(2 * j + 1, sb_ref)
        finish(jnp.maximum(2 * j - 1, 0), pending)
        pending = softmax_pv(sa_ref, m_a)
        m_a = scores(2 * j + 2, sa_ref)
        finish(2 * j, pending)
        return m_a, softmax_pv(sb_ref, m_b)

    assert n_qt % 2 == 0 and n_qt >= 4
    placeholder = (jnp.ones((1, 2 * B_TQ), F32), jnp.zeros((2 * HEAD_DIM, 2 * B_TQ), F32))
    m_a, pending = lax.fori_loop(0, n_qt // 2 - 1, tile_pair, (scores(0, sa_ref), placeholder))
    m_b = scores(n_qt - 1, sb_ref)
    finish(n_qt - 3, pending)
    finish(n_qt - 2, softmax_pv(sa_ref, m_a))
    finish(n_qt - 1, softmax_pv(sb_ref, m_b))


def _diff_attention(qkb, vbt, lam_params, subln_g):
    b, s, _ = qkb.shape
    return pl.pallas_call(
        functools.partial(_diff_kernel, seq=s),
        grid=(b, B_HEADS),
        in_specs=[
            pl.BlockSpec((1, s, LANES), lambda i, h: (i, 0, h)),
            pl.BlockSpec((1, s, LANES), lambda i, h: (i, 0, B_HEADS + h)),
            pl.BlockSpec((1, s // B_TK, LANES, B_TK), lambda i, h: (i, 0, h, 0)),
            pl.BlockSpec((4, HEAD_DIM), lambda i, h: (0, 0)),
            pl.BlockSpec((1, 2 * HEAD_DIM), lambda i, h: (0, 0)),
        ],
        out_specs=pl.BlockSpec((1, s, LANES), lambda i, h: (i, 0, h)),
        out_shape=jax.ShapeDtypeStruct((b, s, B_WIDTH), BF16),
        scratch_shapes=[pltpu.VMEM((s, 2 * B_TQ), F32), pltpu.VMEM((s, 2 * B_TQ), F32)],
        compiler_params=pltpu.CompilerParams(
            dimension_semantics=("arbitrary", "arbitrary"), vmem_limit_bytes=VMEM_LIMIT_BYTES),
        name="diff_attn",
    )(qkb, qkb, vbt, lam_params, subln_g)


MERGE_SUB = 256


def _merge_kernel(x_ref, g_ref, o1_ref, l1_ref, o4_ref, l4_ref, o16_ref, l16_ref, ob_ref,
                  wg_ref, wa_ref, wb_ref, wo_ref, out_ref, s0_ref, s1_ref, s2_ref, s3_ref):
    tm, d = x_ref.shape[1:]

    def token_order(src_ref, dil, scr_ref, t0, nt):
        n, i0 = nt // dil, t0 // dil
        for r in range(dil):
            blk = src_ref[0, r, i0:i0 + n, :].astype(F32)
            for j in range(A_WIDTH // LANES):
                scr_ref[j, pl.ds(t0 + r, n, stride=dil), :] = blk[:, j * LANES:(j + 1) * LANES]
        return jnp.concatenate([scr_ref[j, t0:t0 + nt, :] for j in range(A_WIDTH // LANES)], axis=-1)

    for t0 in range(0, tm, MERGE_SUB):
        rows = slice(t0, t0 + MERGE_SUB)
        x = x_ref[0, rows, :]
        hb = _rms_norm(x, g_ref[...], NORM_EPS).astype(BF16)
        o4, l4 = (token_order(o4_ref, 4, s0_ref, t0, MERGE_SUB), token_order(l4_ref, 4, s1_ref, t0, MERGE_SUB))
        o16, l16 = (token_order(o16_ref, 16, s2_ref, t0, MERGE_SUB), token_order(l16_ref, 16, s3_ref, t0, MERGE_SUB))
        outs = [o1_ref[0, 0, rows, :].astype(F32), o4, o16]
        lses = [l1_ref[0, 0, rows, :], l4, l16]

        m = jnp.maximum(jnp.maximum(lses[0], lses[1]), lses[2])
        es = [jnp.exp2(v - m) for v in lses]
        num = es[0] * outs[0] + es[1] * outs[1] + es[2] * outs[2]
        attn_a = (num * (1.0 / (es[0] + es[1] + es[2]))).astype(BF16)

        ya = jnp.dot(attn_a, wa_ref[...], preferred_element_type=F32)
        yb = jnp.dot(ob_ref[0, rows, :], wb_ref[...], preferred_element_type=F32)
        ga = jnp.dot(hb, wg_ref[:, :d], preferred_element_type=F32)
        gb = jnp.dot(hb, wg_ref[:, d:], preferred_element_type=F32)
        merged = jax.nn.sigmoid(ga) * ya + jax.nn.sigmoid(gb) * yb
        out_ref[0, rows, :] = x + jnp.dot(merged.astype(BF16), wo_ref[...], preferred_element_type=F32)


def _merge(x, g, o1, l1, o4, l4, o16, l16, ob, wg, wa, wb, wo):
    b, s, d = x.shape
    tm = TOKEN_TILE
    tok = lambda w: pl.BlockSpec((1, tm, w), lambda i, j: (i, j, 0))
    phase = lambda dil, w: pl.BlockSpec((1, dil, tm // dil, w), lambda i, j: (i, 0, j, 0))
    const = lambda a: pl.BlockSpec(a.shape, lambda i, j: (0,) * a.ndim)
    return pl.pallas_call(
        _merge_kernel,
        grid=(b, s // tm),
        in_specs=[tok(d), const(g), phase(1, A_WIDTH), phase(1, A_WIDTH), phase(4, A_WIDTH), phase(4, A_WIDTH),
                  phase(16, A_WIDTH), phase(16, A_WIDTH), tok(B_WIDTH),
                  const(wg), const(wa), const(wb), const(wo)],
        out_specs=tok(d),
        out_shape=jax.ShapeDtypeStruct((b, s, d), F32),
        scratch_shapes=[pltpu.VMEM((A_WIDTH // LANES, tm, LANES), F32)] * 4,
        compiler_params=pltpu.CompilerParams(
            dimension_semantics=("arbitrary", "arbitrary"), vmem_limit_bytes=VMEM_LIMIT_BYTES),
        name="merge_proj",
    )(x, g, o1, l1, o4, l4, o16, l16, ob, wg, wa, wb, wo)


MLP_TF = 512
MLP_TM = 1024


def _mlp_kernel(x_ref, g_ref, w1_ref, w2_ref, gf_ref, out_ref):
    x = x_ref[...]
    hb = _rms_norm(x, g_ref[...], NORM_EPS).astype(BF16)
    acc = x
    for c in range(w1_ref.shape[1] // MLP_TF):
        cs = slice(c * MLP_TF, (c + 1) * MLP_TF)
        u = jnp.maximum(jnp.dot(hb, w1_ref[:, cs], preferred_element_type=F32), 0.0)
        acc = acc + jnp.dot((u * u).astype(BF16), w2_ref[cs, :], preferred_element_type=F32)
    out_ref[...] = _rms_norm(acc, gf_ref[...], NORM_EPS)


def _mlp(x2d, g, w1, w2, gf):
    t, d = x2d.shape
    tm = MLP_TM
    row = pl.BlockSpec((tm, d), lambda i: (i, 0))
    full = lambda a: pl.BlockSpec(a.shape, lambda i: (0, 0), pipeline_mode=pl.Buffered(1))
    return pl.pallas_call(
        _mlp_kernel,
        grid=(t // tm,),
        in_specs=[row, full(g), full(w1), full(w2), full(gf)],
        out_specs=row,
        out_shape=jax.ShapeDtypeStruct((t, d), F32),
        compiler_params=pltpu.CompilerParams(
            dimension_semantics=("arbitrary",), vmem_limit_bytes=VMEM_LIMIT_BYTES),
        name="mlp_norm",
    )(x2d, g, w1, w2, gf)


def _rope_tables(seq):
    pos = jnp.arange(seq, dtype=F32)
    inv_freq = ROPE_THETA ** (-jnp.arange(0, HEAD_DIM, 2, dtype=F32) / HEAD_DIM)
    ang = pos[:, None] * inv_freq[None, :]
    cos = jnp.cos(ang)
    sin = jnp.sin(ang)
    cos = jnp.concatenate([cos, cos, cos, cos], axis=-1)
    sin = jnp.concatenate([-sin, sin, -sin, sin], axis=-1)
    return cos, sin


def kernel(x, w_in, w_branch_a, w_branch_b, w_out, lambda_q1, lambda_k1, lambda_q2, lambda_k2,
           diff_subln_g, norm_mix_g, norm_mlp_g, w_ff1, w_ff2, norm_final_g):
    b, s, d = x.shape
    assert w_in.shape[0] == 1, "single-layer block: the final norm is fused into the MLP kernel"
    assert s % B_KB == 0 and s % MLP_TM == 0 and s % QKV_TS == 0
    cos, sin = _rope_tables(s)
    w_in_l = w_in[0].astype(BF16)
    g_mix = norm_mix_g[0].reshape(1, d)
    vb_lo = QKV_A_WIDTH + QK_B_WIDTH
    qkva, qkb, vbt, ph4, ph16 = _qkv_proj(
        x, g_mix, w_in_l[:, :vb_lo], w_in_l[:, vb_lo:QKV_WIDTH].T, cos, sin,
        jnp.asarray(_phase_perm(4), BF16), jnp.asarray(_phase_perm(16), BF16))
    o1, l1 = _band_attention(qkva.reshape(b, 1, s, QKV_A_WIDTH), DILATED_PATTERNS[0][0])
    o4, l4 = _band_attention(ph4, DILATED_PATTERNS[1][0])
    o16, l16 = _band_attention(ph16, DILATED_PATTERNS[2][0])
    lam_params = jnp.stack([lambda_q1[0], lambda_k1[0], lambda_q2[0], lambda_k2[0]]).astype(F32)
    ob = _diff_attention(qkb, vbt, lam_params, diff_subln_g[0].reshape(1, 2 * HEAD_DIM).astype(F32))
    x1 = _merge(x, g_mix, o1, l1, o4, l4, o16, l16, ob, w_in_l[:, QKV_WIDTH:],
                w_branch_a[0].astype(BF16), w_branch_b[0].astype(BF16), w_out[0].astype(BF16))
    out = _mlp(x1.reshape(b * s, d), norm_mlp_g[0].reshape(1, d), w_ff1[0].astype(BF16),
               w_ff2[0].astype(BF16), norm_final_g.reshape(1, d))
    return out.reshape(b, s, d)
```

```python
import functools
import math

import numpy as np
import jax
import jax.numpy as jnp
from jax import lax
from jax.experimental import pallas as pl
from jax.experimental.pallas import tpu as pltpu

HEAD_DIM = 64
HALF_DIM = HEAD_DIM // 2
LANES = 128
A_HEADS = 8
A_WIDTH = A_HEADS * HEAD_DIM
B_HEADS = 4
B_WIDTH = B_HEADS * 2 * HEAD_DIM
QKV_A_WIDTH = 3 * A_WIDTH
QK_B_WIDTH = 2 * B_WIDTH
QKV_WIDTH = QKV_A_WIDTH + QK_B_WIDTH + B_WIDTH
DILATED_PATTERNS = ((128, 1), (512, 4), (2048, 16))
ROPE_THETA = 10000.0
NORM_EPS = 1e-6
SUBLN_EPS = 1e-5
LAMBDA_INIT = 0.8 - 0.6 * math.exp(-0.3 * 0)
QK_SCALE_LOG2E = math.log2(math.e) / math.sqrt(HEAD_DIM)
MASK_VALUE = -1e30
VMEM_LIMIT_BYTES = 56 * 1024 * 1024
PERM = 256
TOKEN_TILE = 512
QKV_TS = 1024

BF16 = jnp.bfloat16
F32 = jnp.float32
NT_DIMS = (((1,), (1,)), ((), ()))


def _rms_norm(x, g, eps):
    return (x * lax.rsqrt(jnp.mean(x * x, axis=-1, keepdims=True) + eps)) * g


def _split_heads_rows(q):
    lane = lax.broadcasted_iota(jnp.int32, q.shape, 1)
    zero = jnp.zeros_like(q)
    return jnp.concatenate(
        [jnp.where(lane < HEAD_DIM, q, zero), jnp.where(lane >= HEAD_DIM, q, zero)], axis=0)


def _phase_perm(dil):
    n = PERM // dil
    p = np.zeros((PERM, PERM), np.float32)
    for r in range(dil):
        for i in range(n):
            p[r * n + i, i * dil + r] = 1.0
    return p


def _qkv_kernel(x_ref, g_ref, w_ref, wvt_ref, cos_ref, sin_ref, p4_ref, p16_ref,
                qkva_ref, qkb_ref, vbt_ref, ph4_ref, ph16_ref):
    x = x_ref[0]
    ts = x.shape[0]
    hb = _rms_norm(x, g_ref[...], NORM_EPS).astype(BF16)
    cos = cos_ref[...]
    sin = sin_ref[...]
    lane = lax.broadcasted_iota(jnp.int32, cos.shape, 1)
    first_half = (lane % HEAD_DIM) < HALF_DIM

    def rope(t, scale):
        parts = []
        for j in range(t.shape[1] // LANES):
            tj = t[:, j * LANES:(j + 1) * LANES]
            rot = jnp.where(first_half, pltpu.roll(tj, LANES - HALF_DIM, 1), pltpu.roll(tj, HALF_DIM, 1))
            tj = tj * cos + rot * sin
            parts.append((tj * scale if scale != 1.0 else tj).astype(BF16))
        return parts

    for c in range(5):
        t = jnp.dot(hb, w_ref[:, c * A_WIDTH:(c + 1) * A_WIDTH], preferred_element_type=F32)
        dst, base = (qkva_ref, c * A_WIDTH) if c < 3 else (qkb_ref, (c - 3) * A_WIDTH)
        if c == 2:
            dst[0, :, base:base + A_WIDTH] = t.astype(BF16)
        else:
            scale = QK_SCALE_LOG2E if c in (0, 3) else 1.0
            for j, tj in enumerate(rope(t, scale)):
                dst[0, :, base + j * LANES:base + (j + 1) * LANES] = tj
    vbt = lax.dot_general(wvt_ref[...], hb, NT_DIMS, preferred_element_type=F32).astype(BF16)
    for u in range(ts // B_TK):
        vbt_ref[0, u] = vbt[:, u * B_TK:(u + 1) * B_TK]

    for dil, p_ref, ph_ref in ((4, p4_ref, ph4_ref), (16, p16_ref, ph16_ref)):
        n = PERM // dil
        for u in range(ts // PERM):
            for c in range(3):
                cs = slice(c * A_WIDTH, (c + 1) * A_WIDTH)
                tb = qkva_ref[0, u * PERM:(u + 1) * PERM, cs]
                ph = jnp.dot(p_ref[...], tb, preferred_element_type=F32).astype(BF16)
                for r in range(dil):
                    ph_ref[0, r, u * n:(u + 1) * n, cs] = ph[r * n:(r + 1) * n]


def _qkv_proj(x, g, w_qkv, w_vbt, cos, sin, p4, p16):
    b, s, d = x.shape
    ts = QKV_TS
    const = lambda a: pl.BlockSpec(a.shape, lambda i, j: (0,) * a.ndim, pipeline_mode=pl.Buffered(1))
    return pl.pallas_call(
        _qkv_kernel,
        grid=(b, s // ts),
        in_specs=[
            pl.BlockSpec((1, ts, d), lambda i, j: (i, j, 0)),
            const(g), const(w_qkv), const(w_vbt),
            pl.BlockSpec((ts, LANES), lambda i, j: (j, 0)),
            pl.BlockSpec((ts, LANES), lambda i, j: (j, 0)),
            const(p4), const(p16),
        ],
        out_specs=[
            pl.BlockSpec((1, ts, QKV_A_WIDTH), lambda i, j: (i, j, 0)),
            pl.BlockSpec((1, ts, QK_B_WIDTH), lambda i, j: (i, j, 0)),
            pl.BlockSpec((1, ts // B_TK, B_WIDTH, B_TK), lambda i, j: (i, j, 0, 0)),
            pl.BlockSpec((1, 4, ts // 4, QKV_A_WIDTH), lambda i, j: (i, 0, j, 0)),
            pl.BlockSpec((1, 16, ts // 16, QKV_A_WIDTH), lambda i, j: (i, 0, j, 0)),
        ],
        out_shape=[
            jax.ShapeDtypeStruct((b, s, QKV_A_WIDTH), BF16),
            jax.ShapeDtypeStruct((b, s, QK_B_WIDTH), BF16),
            jax.ShapeDtypeStruct((b, s // B_TK, B_WIDTH, B_TK), BF16),
            jax.ShapeDtypeStruct((b, 4, s // 4, QKV_A_WIDTH), BF16),
            jax.ShapeDtypeStruct((b, 16, s // 16, QKV_A_WIDTH), BF16),
        ],
        compiler_params=pltpu.CompilerParams(
            dimension_semantics=("arbitrary", "arbitrary"), vmem_limit_bytes=VMEM_LIMIT_BYTES),
        name="qkv_proj",
    )(x, g, w_qkv, w_vbt, cos, sin, p4, p16)


A_TQ = 128
A_WIN = 256
BAND_ROWS = 2048


def _band_bias():
    row = np.arange(2 * A_TQ)[:, None] % A_TQ
    col = np.arange(A_WIN)[None, :]
    half = (A_WIN - A_TQ) // 2
    return np.stack([np.where(np.abs(col - row + d) <= half, 0.0, MASK_VALUE)
                     for d in (-half, 0, -2 * half)]).astype(np.float32)


def _band_kernel(q_ref, k_ref, v_ref, bias_ref, o_ref, lse_ref, *, half, ls, rows_per_step):
    step = pl.program_id(2)
    n_sub = rows_per_step // A_TQ
    n_phase = q_ref.shape[1]
    out_lane = lax.broadcasted_iota(jnp.int32, (A_TQ, LANES), 1)

    def sub_tile(u, carry):
        ph = u // n_sub
        r0 = pl.multiple_of((u % n_sub) * A_TQ, A_TQ)
        t0 = step * rows_per_step + r0
        start = pl.multiple_of(jnp.clip(t0 - half, 0, ls - A_WIN), half)
        which = jnp.where(t0 == 0, 1, jnp.where(t0 == ls - A_TQ, 2, 0))
        for j in range(A_WIDTH // LANES):
            cs = slice(j * LANES, (j + 1) * LANES)
            qq = _split_heads_rows(q_ref[0, ph, pl.ds(r0, A_TQ), cs])
            kw = k_ref[0, ph, pl.ds(start, A_WIN), cs]
            vw = v_ref[0, ph, pl.ds(start, A_WIN), cs]
            s = lax.dot_general(qq, kw, NT_DIMS, preferred_element_type=F32) + bias_ref[which]
            m = jnp.max(s, axis=-1, keepdims=True)
            p = jnp.exp2(s - m)
            l = jnp.sum(p, axis=-1, keepdims=True)
            pv = jnp.dot(p.astype(BF16), vw, preferred_element_type=F32) * (1.0 / l)
            o_ref[0, ph, pl.ds(r0, A_TQ), cs] = jnp.where(
                out_lane < HEAD_DIM, pv[:A_TQ], pv[A_TQ:]).astype(BF16)
            lse = m + jnp.log2(l)
            lse_ref[0, ph, pl.ds(r0, A_TQ), cs] = jnp.where(out_lane < HEAD_DIM, lse[:A_TQ], lse[A_TQ:])
        return carry

    lax.fori_loop(0, n_phase * n_sub, sub_tile, 0, unroll=16)


def _band_attention(ph, window):
    b, dil, ls, _ = ph.shape
    half = window // (2 * dil)
    assert half == A_TQ // 2 and A_WIN == A_TQ + 2 * half and ls % A_TQ == 0 and ls >= A_WIN
    rows_per_step = min(ls, BAND_ROWS)
    n_phase = BAND_ROWS // rows_per_step
    assert dil % n_phase == 0
    kern = functools.partial(_band_kernel, half=half, ls=ls, rows_per_step=rows_per_step)
    return pl.pallas_call(
        kern,
        grid=(b, dil // n_phase, ls // rows_per_step),
        in_specs=[
            pl.BlockSpec((1, n_phase, rows_per_step, A_WIDTH), lambda i, r, t: (i, r, t, 0)),
            pl.BlockSpec((1, n_phase, ls, A_WIDTH), lambda i, r, t: (i, r, 0, 1)),
            pl.BlockSpec((1, n_phase, ls, A_WIDTH), lambda i, r, t: (i, r, 0, 2)),
            pl.BlockSpec((3, 2 * A_TQ, A_WIN), lambda i, r, t: (0, 0, 0)),
        ],
        out_specs=[
            pl.BlockSpec((1, n_phase, rows_per_step, A_WIDTH), lambda i, r, t: (i, r, t, 0)),
            pl.BlockSpec((1, n_phase, rows_per_step, A_WIDTH), lambda i, r, t: (i, r, t, 0)),
        ],
        out_shape=[
            jax.ShapeDtypeStruct((b, dil, ls, A_WIDTH), BF16),
            jax.ShapeDtypeStruct((b, dil, ls, A_WIDTH), F32),
        ],
        compiler_params=pltpu.CompilerParams(
            dimension_semantics=("arbitrary", "arbitrary", "arbitrary"),
            vmem_limit_bytes=VMEM_LIMIT_BYTES),
        name=f"band_attn_d{dil}",
    )(ph, ph, ph, jnp.asarray(_band_bias()))


B_TQ = 256
B_KB = 1024
B_TK = 256


def _diff_kernel(q_ref, k_ref, vt_ref, lam_ref, g_ref, o_ref, sa_ref, sb_ref, *, seq):
    lam_p = lam_ref[...]
    lam = (jnp.exp(jnp.sum(lam_p[0:1] * lam_p[1:2], axis=-1, keepdims=True))
           - jnp.exp(jnp.sum(lam_p[2:3] * lam_p[3:4], axis=-1, keepdims=True)) + LAMBDA_INIT)
    g = g_ref[...]

    n_kb = seq // B_KB
    n_qt = seq // B_TQ

    def load_qq(t):
        return _split_heads_rows(q_ref[0, pl.ds(pl.multiple_of(t * B_TQ, B_TQ), B_TQ), :])

    def scores(t, s_ref):
        qq = load_qq(t)
        cmax = None
        for kb in range(n_kb):
            s = lax.dot_general(k_ref[0, kb * B_KB:(kb + 1) * B_KB, :], qq, NT_DIMS,
                                preferred_element_type=F32)
            s_ref[kb * B_KB:(kb + 1) * B_KB, :] = s
            bmax = jnp.max(s, axis=0, keepdims=True)
            cmax = bmax if cmax is None else jnp.maximum(cmax, bmax)
        return cmax

    def softmax_pv(s_ref, m):
        l = jnp.zeros((1, 2 * B_TQ), F32)
        acc = jnp.zeros((2 * HEAD_DIM, 2 * B_TQ), F32)
        for c in range(seq // B_TK):
            p = jnp.exp2(s_ref[c * B_TK:(c + 1) * B_TK, :] - m)
            l = l + jnp.sum(p, axis=0, keepdims=True)
            acc = acc + jnp.dot(vt_ref[0, c], p.astype(BF16), preferred_element_type=F32)
        return l, acc

    def finish(t, l_acc):
        l, acc = l_acc
        o = acc * (1.0 / l)
        o = o[:, :B_TQ] - lam * o[:, B_TQ:]
        o = o * lax.rsqrt(jnp.mean(o * o, axis=0, keepdims=True) + SUBLN_EPS)
        q0 = pl.multiple_of(t * B_TQ, B_TQ)
        o_ref[0, pl.ds(q0, B_TQ), :] = ((o.T * g) * (1.0 - LAMBDA_INIT)).astype(BF16)

    def tile_pair(j, carry):
        m_a, pending = carry
        m_b = scores(2 * j + 1, sb_ref)
        finish(jnp.maximum(2 * j - 1, 0), pending)
        pending = softmax_pv(sa_ref, m_a)
        m_a = scores(2 * j + 2, sa_ref)
        finish(2 * j, pending)
        return m_a, softmax_pv(sb_ref, m_b)

    assert n_qt % 2 == 0 and n_qt >= 4
    placeholder = (jnp.ones((1, 2 * B_TQ), F32), jnp.zeros((2 * HEAD_DIM, 2 * B_TQ), F32))
    m_a, pending = lax.fori_loop(0, n_qt // 2 - 1, tile_pair, (scores(0, sa_ref), placeholder))
    m_b = scores(n_qt - 1, sb_ref)
    finish(n_qt - 3, pending)
    finish(n_qt - 2, softmax_pv(sa_ref, m_a))
    finish(n_qt - 1, softmax_pv(sb_ref, m_b))


def _diff_attention(qkb, vbt, lam_params, subln_g):
    b, s, _ = qkb.shape
    return pl.pallas_call(
        functools.partial(_diff_kernel, seq=s),
        grid=(b, B_HEADS),
        in_specs=[
            pl.BlockSpec((1, s, LANES), lambda i, h: (i, 0, h)),
            pl.BlockSpec((1, s, LANES), lambda i, h: (i, 0, B_HEADS + h)),
            pl.BlockSpec((1, s // B_TK, LANES, B_TK), lambda i, h: (i, 0, h, 0)),
            pl.BlockSpec((4, HEAD_DIM), lambda i, h: (0, 0)),
            pl.BlockSpec((1, 2 * HEAD_DIM), lambda i, h: (0, 0)),
        ],
        out_specs=pl.BlockSpec((1, s, LANES), lambda i, h: (i, 0, h)),
        out_shape=jax.ShapeDtypeStruct((b, s, B_WIDTH), BF16),
        scratch_shapes=[pltpu.VMEM((s, 2 * B_TQ), F32), pltpu.VMEM((s, 2 * B_TQ), F32)],
        compiler_params=pltpu.CompilerParams(
            dimension_semantics=("arbitrary", "arbitrary"), vmem_limit_bytes=VMEM_LIMIT_BYTES),
        name="diff_attn",
    )(qkb, qkb, vbt, lam_params, subln_g)


MERGE_SUB = 256


def _merge_kernel(x_ref, g_ref, o1_ref, l1_ref, o4_ref, l4_ref, o16_ref, l16_ref, ob_ref,
                  wg_ref, wa_ref, wb_ref, wo_ref, out_ref, s0_ref, s1_ref, s2_ref, s3_ref):
    tm, d = x_ref.shape[1:]

    def token_order(src_ref, dil, scr_ref, t0, nt):
        n, i0 = nt // dil, t0 // dil
        for r in range(dil):
            blk = src_ref[0, r, i0:i0 + n, :].astype(F32)
            for j in range(A_WIDTH // LANES):
                scr_ref[j, pl.ds(t0 + r, n, stride=dil), :] = blk[:, j * LANES:(j + 1) * LANES]
        return jnp.concatenate([scr_ref[j, t0:t0 + nt, :] for j in range(A_WIDTH // LANES)], axis=-1)

    for t0 in range(0, tm, MERGE_SUB):
        rows = slice(t0, t0 + MERGE_SUB)
        x = x_ref[0, rows, :]
        hb = _rms_norm(x, g_ref[...], NORM_EPS).astype(BF16)
        o4, l4 = (token_order(o4_ref, 4, s0_ref, t0, MERGE_SUB), token_order(l4_ref, 4, s1_ref, t0, MERGE_SUB))
        o16, l16 = (token_order(o16_ref, 16, s2_ref, t0, MERGE_SUB), token_order(l16_ref, 16, s3_ref, t0, MERGE_SUB))
        outs = [o1_ref[0, 0, rows, :].astype(F32), o4, o16]
        lses = [l1_ref[0, 0, rows, :], l4, l16]

        m = jnp.maximum(jnp.maximum(lses[0], lses[1]), lses[2])
        es = [jnp.exp2(v - m) for v in lses]
        num = es[0] * outs[0] + es[1] * outs[1] + es[2] * outs[2]
        attn_a = (num * (1.0 / (es[0] + es[1] + es[2]))).astype(BF16)

        ya = jnp.dot(attn_a, wa_ref[...], preferred_element_type=F32)
        yb = jnp.dot(ob_ref[0, rows, :], wb_ref[...], preferred_element_type=F32)
        ga = jnp.dot(hb, wg_ref[:, :d], preferred_element_type=F32)
        gb = jnp.dot(hb, wg_ref[:, d:], preferred_element_type=F32)
        merged = jax.nn.sigmoid(ga) * ya + jax.nn.sigmoid(gb) * yb
        out_ref[0, rows, :] = x + jnp.dot(merged.astype(BF16), wo_ref[...], preferred_element_type=F32)


def _merge(x, g, o1, l1, o4, l4, o16, l16, ob, wg, wa, wb, wo):
    b, s, d = x.shape
    tm = TOKEN_TILE
    tok = lambda w: pl.BlockSpec((1, tm, w), lambda i, j: (i, j, 0))
    phase = lambda dil, w: pl.BlockSpec((1, dil, tm // dil, w), lambda i, j: (i, 0, j, 0))
    const = lambda a: pl.BlockSpec(a.shape, lambda i, j: (0,) * a.ndim)
    return pl.pallas_call(
        _merge_kernel,
        grid=(b, s // tm),
        in_specs=[tok(d), const(g), phase(1, A_WIDTH), phase(1, A_WIDTH), phase(4, A_WIDTH), phase(4, A_WIDTH),
                  phase(16, A_WIDTH), phase(16, A_WIDTH), tok(B_WIDTH),
                  const(wg), const(wa), const(wb), const(wo)],
        out_specs=tok(d),
        out_shape=jax.ShapeDtypeStruct((b, s, d), F32),
        scratch_shapes=[pltpu.VMEM((A_WIDTH // LANES, tm, LANES), F32)] * 4,
        compiler_params=pltpu.CompilerParams(
            dimension_semantics=("arbitrary", "arbitrary"), vmem_limit_bytes=VMEM_LIMIT_BYTES),
        name="merge_proj",
    )(x, g, o1, l1, o4, l4, o16, l16, ob, wg, wa, wb, wo)


MLP_TF = 512
MLP_TM = 1024


def _mlp_kernel(x_ref, g_ref, w1_ref, w2_ref, gf_ref, out_ref):
    x = x_ref[...]
    hb = _rms_norm(x, g_ref[...], NORM_EPS).astype(BF16)
    acc = x
    for c in range(w1_ref.shape[1] // MLP_TF):
        cs = slice(c * MLP_TF, (c + 1) * MLP_TF)
        u = jnp.maximum(jnp.dot(hb, w1_ref[:, cs], preferred_element_type=F32), 0.0)
        acc = acc + jnp.dot((u * u).astype(BF16), w2_ref[cs, :], preferred_element_type=F32)
    out_ref[...] = _rms_norm(acc, gf_ref[...], NORM_EPS)


def _mlp(x2d, g, w1, w2, gf):
    t, d = x2d.shape
    tm = MLP_TM
    row = pl.BlockSpec((tm, d), lambda i: (i, 0))
    full = lambda a: pl.BlockSpec(a.shape, lambda i: (0, 0), pipeline_mode=pl.Buffered(1))
    return pl.pallas_call(
        _mlp_kernel,
        grid=(t // tm,),
        in_specs=[row, full(g), full(w1), full(w2), full(gf)],
        out_specs=row,
        out_shape=jax.ShapeDtypeStruct((t, d), F32),
        compiler_params=pltpu.CompilerParams(
            dimension_semantics=("arbitrary",), vmem_limit_bytes=VMEM_LIMIT_BYTES),
        name="mlp_norm",
    )(x2d, g, w1, w2, gf)


def _rope_tables(seq):
    pos = jnp.arange(seq, dtype=F32)
    inv_freq = ROPE_THETA ** (-jnp.arange(0, HEAD_DIM, 2, dtype=F32) / HEAD_DIM)
    ang = pos[:, None] * inv_freq[None, :]
    cos = jnp.cos(ang)
    sin = jnp.sin(ang)
    cos = jnp.concatenate([cos, cos, cos, cos], axis=-1)
    sin = jnp.concatenate([-sin, sin, -sin, sin], axis=-1)
    return cos, sin


def kernel(x, w_in, w_branch_a, w_branch_b, w_out, lambda_q1, lambda_k1, lambda_q2, lambda_k2,
           diff_subln_g, norm_mix_g, norm_mlp_g, w_ff1, w_ff2, norm_final_g):
    b, s, d = x.shape
    assert w_in.shape[0] == 1, "single-layer block: the final norm is fused into the MLP kernel"
    assert s % B_KB == 0 and s % MLP_TM == 0 and s % QKV_TS == 0
    cos, sin = _rope_tables(s)
    w_in_l = w_in[0].astype(BF16)
    g_mix = norm_mix_g[0].reshape(1, d)
    vb_lo = QKV_A_WIDTH + QK_B_WIDTH
    qkva, qkb, vbt, ph4, ph16 = _qkv_proj(
        x, g_mix, w_in_l[:, :vb_lo], w_in_l[:, vb_lo:QKV_WIDTH].T, cos, sin,
        jnp.asarray(_phase_perm(4), BF16), jnp.asarray(_phase_perm(16), BF16))
    o1, l1 = _band_attention(qkva.reshape(b, 1, s, QKV_A_WIDTH), DILATED_PATTERNS[0][0])
    o4, l4 = _band_attention(ph4, DILATED_PATTERNS[1][0])
    o16, l16 = _band_attention(ph16, DILATED_PATTERNS[2][0])
    lam_params = jnp.stack([lambda_q1[0], lambda_k1[0], lambda_q2[0], lambda_k2[0]]).astype(F32)
    ob = _diff_attention(qkb, vbt, lam_params, diff_subln_g[0].reshape(1, 2 * HEAD_DIM).astype(F32))
    x1 = _merge(x, g_mix, o1, l1, o4, l4, o16, l16, ob, w_in_l[:, QKV_WIDTH:],
                w_branch_a[0].astype(BF16), w_branch_b[0].astype(BF16), w_out[0].astype(BF16))
    out = _mlp(x1.reshape(b * s, d), norm_mlp_g[0].reshape(1, d), w_ff1[0].astype(BF16),
               w_ff2[0].astype(BF16), norm_final_g.reshape(1, d))
    return out.reshape(b, s, d)
```

```python
import functools
import math

import numpy as np
import jax
import jax.numpy as jnp
from jax import lax
from jax.experimental import pallas as pl
from jax.experimental.pallas import tpu as pltpu

HEAD_DIM = 64
HALF_DIM = HEAD_DIM // 2
LANES = 128
A_HEADS = 8
A_WIDTH = A_HEADS * HEAD_DIM
B_HEADS = 4
B_WIDTH = B_HEADS * 2 * HEAD_DIM
QKV_A_WIDTH = 3 * A_WIDTH
QK_B_WIDTH = 2 * B_WIDTH
QKV_WIDTH = QKV_A_WIDTH + QK_B_WIDTH + B_WIDTH
DILATED_PATTERNS = ((128, 1), (512, 4), (2048, 16))
ROPE_THETA = 10000.0
NORM_EPS = 1e-6
SUBLN_EPS = 1e-5
LAMBDA_INIT = 0.8 - 0.6 * math.exp(-0.3 * 0)
QK_SCALE_LOG2E = math.log2(math.e) / math.sqrt(HEAD_DIM)
MASK_VALUE = -math.inf
VMEM_LIMIT_BYTES = 56 * 1024 * 1024
PERM = 256
TOKEN_TILE = 512
QKV_TS = 1024

BF16 = jnp.bfloat16
F32 = jnp.float32
NT_DIMS = (((1,), (1,)), ((), ()))


def _rms_norm(x, g, eps):
    return (x * lax.rsqrt(jnp.mean(x * x, axis=-1, keepdims=True) + eps)) * g


def _split_heads_rows(q):
    lane = lax.broadcasted_iota(jnp.int32, q.shape, 1)
    zero = jnp.zeros_like(q)
    return jnp.concatenate(
        [jnp.where(lane < HEAD_DIM, q, zero), jnp.where(lane >= HEAD_DIM, q, zero)], axis=0)


def _phase_perm(dil):
    n = PERM // dil
    p = np.zeros((PERM, PERM), np.float32)
    for r in range(dil):
        for i in range(n):
            p[r * n + i, i * dil + r] = 1.0
    return p


def _qkv_kernel(x_ref, g_ref, w_ref, wvt_ref, cos_ref, sin_ref, p4_ref, p16_ref,
                qkva_ref, qkb_ref, vbt_ref, ph4_ref, ph16_ref):
    x = x_ref[0]
    ts = x.shape[0]
    hb = _rms_norm(x, g_ref[...], NORM_EPS).astype(BF16)
    cos = cos_ref[...]
    sin = sin_ref[...]
    lane = lax.broadcasted_iota(jnp.int32, cos.shape, 1)
    first_half = (lane % HEAD_DIM) < HALF_DIM

    def rope(t, scale):
        parts = []
        for j in range(t.shape[1] // LANES):
            tj = t[:, j * LANES:(j + 1) * LANES]
            rot = jnp.where(first_half, pltpu.roll(tj, LANES - HALF_DIM, 1), pltpu.roll(tj, HALF_DIM, 1))
            tj = tj * cos + rot * sin
            parts.append((tj * scale if scale != 1.0 else tj).astype(BF16))
        return parts

    for c in range(5):
        t = jnp.dot(hb, w_ref[:, c * A_WIDTH:(c + 1) * A_WIDTH], preferred_element_type=F32)
        dst, base = (qkva_ref, c * A_WIDTH) if c < 3 else (qkb_ref, (c - 3) * A_WIDTH)
        if c == 2:
            dst[0, :, base:base + A_WIDTH] = t.astype(BF16)
        else:
            scale = QK_SCALE_LOG2E if c in (0, 3) else 1.0
            for j, tj in enumerate(rope(t, scale)):
                dst[0, :, base + j * LANES:base + (j + 1) * LANES] = tj
    vbt = lax.dot_general(wvt_ref[...], hb, NT_DIMS, preferred_element_type=F32).astype(BF16)
    for u in range(ts // B_TK):
        vbt_ref[0, u] = vbt[:, u * B_TK:(u + 1) * B_TK]

    for dil, p_ref, ph_ref in ((4, p4_ref, ph4_ref), (16, p16_ref, ph16_ref)):
        n = PERM // dil
        for u in range(ts // PERM):
            for c in range(3):
                cs = slice(c * A_WIDTH, (c + 1) * A_WIDTH)
                tb = qkva_ref[0, u * PERM:(u + 1) * PERM, cs]
                ph = jnp.dot(p_ref[...], tb, preferred_element_type=F32).astype(BF16)
                for r in range(dil):
                    ph_ref[0, r, u * n:(u + 1) * n, cs] = ph[r * n:(r + 1) * n]


def _qkv_proj(x, g, w_qkv, w_vbt, cos, sin, p4, p16):
    b, s, d = x.shape
    ts = QKV_TS
    const = lambda a: pl.BlockSpec(a.shape, lambda i, j: (0,) * a.ndim, pipeline_mode=pl.Buffered(1))
    return pl.pallas_call(
        _qkv_kernel,
        grid=(b, s // ts),
        in_specs=[
            pl.BlockSpec((1, ts, d), lambda i, j: (i, j, 0)),
            const(g), const(w_qkv), const(w_vbt),
            pl.BlockSpec((ts, LANES), lambda i, j: (j, 0)),
            pl.BlockSpec((ts, LANES), lambda i, j: (j, 0)),
            const(p4), const(p16),
        ],
        out_specs=[
            pl.BlockSpec((1, ts, QKV_A_WIDTH), lambda i, j: (i, j, 0)),
            pl.BlockSpec((1, ts, QK_B_WIDTH), lambda i, j: (i, j, 0)),
            pl.BlockSpec((1, ts // B_TK, B_WIDTH, B_TK), lambda i, j: (i, j, 0, 0)),
            pl.BlockSpec((1, 4, ts // 4, QKV_A_WIDTH), lambda i, j: (i, 0, j, 0)),
            pl.BlockSpec((1, 16, ts // 16, QKV_A_WIDTH), lambda i, j: (i, 0, j, 0)),
        ],
        out_shape=[
            jax.ShapeDtypeStruct((b, s, QKV_A_WIDTH), BF16),
            jax.ShapeDtypeStruct((b, s, QK_B_WIDTH), BF16),
            jax.ShapeDtypeStruct((b, s // B_TK, B_WIDTH, B_TK), BF16),
            jax.ShapeDtypeStruct((b, 4, s // 4, QKV_A_WIDTH), BF16),
            jax.ShapeDtypeStruct((b, 16, s // 16, QKV_A_WIDTH), BF16),
        ],
        compiler_params=pltpu.CompilerParams(
            dimension_semantics=("arbitrary", "arbitrary"), vmem_limit_bytes=VMEM_LIMIT_BYTES),
        name="qkv_proj",
    )(x, g, w_qkv, w_vbt, cos, sin, p4, p16)


A_TQ = 128
A_WIN = 256
BAND_ROWS = 2048


def _band_bias():
    row = np.arange(2 * A_TQ)[:, None] % A_TQ
    col = np.arange(A_WIN)[None, :]
    half = (A_WIN - A_TQ) // 2
    return np.stack([np.where(np.abs(col - row + d) <= half, 0.0, MASK_VALUE)
                     for d in (-half, 0, -2 * half)]).astype(np.float32)


def _band_kernel(q_ref, k_ref, v_ref, bias_ref, o_ref, lse_ref, *, half, ls, rows_per_step):
    step = pl.program_id(2)
    n_sub = rows_per_step // A_TQ
    n_phase = q_ref.shape[1]
    out_lane = lax.broadcasted_iota(jnp.int32, (A_TQ, LANES), 1)

    def sub_tile(u, carry):
        ph = u // n_sub
        r0 = pl.multiple_of((u % n_sub) * A_TQ, A_TQ)
        t0 = step * rows_per_step + r0
        start = pl.multiple_of(jnp.clip(t0 - half, 0, ls - A_WIN), half)
        which = jnp.where(t0 == 0, 1, jnp.where(t0 == ls - A_TQ, 2, 0))
        for j in range(A_WIDTH // LANES):
            cs = slice(j * LANES, (j + 1) * LANES)
            qq = _split_heads_rows(q_ref[0, ph, pl.ds(r0, A_TQ), cs])
            kw = k_ref[0, ph, pl.ds(start, A_WIN), cs]
            vw = v_ref[0, ph, pl.ds(start, A_WIN), cs]
            s = lax.dot_general(qq, kw, NT_DIMS, preferred_element_type=F32) + bias_ref[which]
            m = jnp.max(s, axis=-1, keepdims=True)
            p = jnp.exp2(s - m)
            l = jnp.sum(p, axis=-1, keepdims=True)
            pv = jnp.dot(p.astype(BF16), vw, preferred_element_type=F32) * (1.0 / l)
            o_ref[0, ph, pl.ds(r0, A_TQ), cs] = jnp.where(
                out_lane < HEAD_DIM, pv[:A_TQ], pv[A_TQ:]).astype(BF16)
            lse = m + jnp.log2(l)
            lse_ref[0, ph, pl.ds(r0, A_TQ), cs] = jnp.where(out_lane < HEAD_DIM, lse[:A_TQ], lse[A_TQ:])
        return carry

    lax.fori_loop(0, n_phase * n_sub, sub_tile, 0, unroll=16)


def _band_attention(ph, window):
    b, dil, ls, _ = ph.shape
    half = window // (2 * dil)
    assert half == A_TQ // 2 and A_WIN == A_TQ + 2 * half and ls % A_TQ == 0 and ls >= A_WIN
    rows_per_step = min(ls, BAND_ROWS)
    n_phase = BAND_ROWS // rows_per_step
    assert dil % n_phase == 0
    kern = functools.partial(_band_kernel, half=half, ls=ls, rows_per_step=rows_per_step)
    return pl.pallas_call(
        kern,
        grid=(b, dil // n_phase, ls // rows_per_step),
        in_specs=[
            pl.BlockSpec((1, n_phase, rows_per_step, A_WIDTH), lambda i, r, t: (i, r, t, 0)),
            pl.BlockSpec((1, n_phase, ls, A_WIDTH), lambda i, r, t: (i, r, 0, 1)),
            pl.BlockSpec((1, n_phase, ls, A_WIDTH), lambda i, r, t: (i, r, 0, 2)),
            pl.BlockSpec((3, 2 * A_TQ, A_WIN), lambda i, r, t: (0, 0, 0)),
        ],
        out_specs=[
            pl.BlockSpec((1, n_phase, rows_per_step, A_WIDTH), lambda i, r, t: (i, r, t, 0)),
            pl.BlockSpec((1, n_phase, rows_per_step, A_WIDTH), lambda i, r, t: (i, r, t, 0)),
        ],
        out_shape=[
            jax.ShapeDtypeStruct((b, dil, ls, A_WIDTH), BF16),
            jax.ShapeDtypeStruct((b, dil, ls, A_WIDTH), F32),
        ],
        compiler_params=pltpu.CompilerParams(
            dimension_semantics=("arbitrary", "arbitrary", "arbitrary"),
            vmem_limit_bytes=VMEM_LIMIT_BYTES),
        name=f"band_attn_d{dil}",
    )(ph, ph, ph, jnp.asarray(_band_bias()))


B_TQ = 256
B_KB = 1024
B_TK = 256


def _diff_kernel(q_ref, k_ref, vt_ref, lam_ref, g_ref, o_ref, sa_ref, sb_ref, *, seq):
    lam_p = lam_ref[...]
    lam = (jnp.exp(jnp.sum(lam_p[0:1] * lam_p[1:2], axis=-1, keepdims=True))
           - jnp.exp(jnp.sum(lam_p[2:3] * lam_p[3:4], axis=-1, keepdims=True)) + LAMBDA_INIT)
    g = g_ref[...]

    n_kb = seq // B_KB
    n_qt = seq // B_TQ

    def load_qq(t):
        return _split_heads_rows(q_ref[0, pl.ds(pl.multiple_of(t * B_TQ, B_TQ), B_TQ), :])

    def scores(t, s_ref):
        qq = load_qq(t)
        cmax = None
        for kb in range(n_kb):
            s = lax.dot_general(k_ref[0, kb * B_KB:(kb + 1) * B_KB, :], qq, NT_DIMS,
                                preferred_element_type=F32)
            s_ref[kb * B_KB:(kb + 1) * B_KB, :] = s
            bmax = jnp.max(s, axis=0, keepdims=True)
            cmax = bmax if cmax is None else jnp.maximum(cmax, bmax)
        return cmax

    def softmax_pv(s_ref, m):
        l = jnp.zeros((1, 2 * B_TQ), F32)
        acc = jnp.zeros((2 * HEAD_DIM, 2 * B_TQ), F32)
        for c in range(seq // B_TK):
            p = jnp.exp2(s_ref[c * B_TK:(c + 1) * B_TK, :] - m)
            l = l + jnp.sum(p, axis=0, keepdims=True)
            acc = acc + jnp.dot(vt_ref[0, c], p.astype(BF16), preferred_element_type=F32)
        return l, acc

    def finish(t, l_acc):
        l, acc = l_acc
        o = acc * (1.0 / l)
        o = o[:, :B_TQ] - lam * o[:, B_TQ:]
        o = o * lax.rsqrt(jnp.mean(o * o, axis=0, keepdims=True) + SUBLN_EPS)
        q0 = pl.multiple_of(t * B_TQ, B_TQ)
        o_ref[0, pl.ds(q0, B_TQ), :] = ((o.T * g) * (1.0 - LAMBDA_INIT)).astype(BF16)

    def tile_pair(j, carry):
        m_a, pending = carry
        m_b = scores(2 * j + 1, sb_ref)
        finish(jnp.maximum(2 * j - 1, 0), pending)
        pending = softmax_pv(sa_ref, m_a)
        m_a = scores(2 * j + 2, sa_ref)
        finish(2 * j, pending)
        return m_a, softmax_pv(sb_ref, m_b)

    assert n_qt % 2 == 0 and n_qt >= 4
    placeholder = (jnp.ones((1, 2 * B_TQ), F32), jnp.zeros((2 * HEAD_DIM, 2 * B_TQ), F32))
    m_a, pending = lax.fori_loop(0, n_qt // 2 - 1, tile_pair, (scores(0, sa_ref), placeholder))
    m_b = scores(n_qt - 1, sb_ref)
    finish(n_qt - 3, pending)
    finish(n_qt - 2, softmax_pv(sa_ref, m_a))
    finish(n_qt - 1, softmax_pv(sb_ref, m_b))


def _diff_attention(qkb, vbt, lam_params, subln_g):
    b, s, _ = qkb.shape
    return pl.pallas_call(
        functools.partial(_diff_kernel, seq=s),
        grid=(b, B_HEADS),
        in_specs=[
            pl.BlockSpec((1, s, LANES), lambda i, h: (i, 0, h)),
            pl.BlockSpec((1, s, LANES), lambda i, h: (i, 0, B_HEADS + h)),
            pl.BlockSpec((1, s // B_TK, LANES, B_TK), lambda i, h: (i, 0, h, 0)),
            pl.BlockSpec((4, HEAD_DIM), lambda i, h: (0, 0)),
            pl.BlockSpec((1, 2 * HEAD_DIM), lambda i, h: (0, 0)),
        ],
        out_specs=pl.BlockSpec((1, s, LANES), lambda i, h: (i, 0, h)),
        out_shape=jax.ShapeDtypeStruct((b, s, B_WIDTH), BF16),
        scratch_shapes=[pltpu.VMEM((s, 2 * B_TQ), F32), pltpu.VMEM((s, 2 * B_TQ), F32)],
        compiler_params=pltpu.CompilerParams(
            dimension_semantics=("arbitrary", "arbitrary"), vmem_limit_bytes=VMEM_LIMIT_BYTES),
        name="diff_attn",
    )(qkb, qkb, vbt, lam_params, subln_g)


MERGE_SUB = 256


def _merge_kernel(x_ref, g_ref, o1_ref, l1_ref, o4_ref, l4_ref, o16_ref, l16_ref, ob_ref,
                  wg_ref, wa_ref, wb_ref, wo_ref, out_ref, s0_ref, s1_ref, s2_ref, s3_ref):
    tm, d = x_ref.shape[1:]

    def token_order(src_ref, dil, scr_ref, t0, nt):
        n, i0 = nt // dil, t0 // dil
        for r in range(dil):
            blk = src_ref[0, r, i0:i0 + n, :].astype(F32)
            for j in range(A_WIDTH // LANES):
                scr_ref[j, pl.ds(t0 + r, n, stride=dil), :] = blk[:, j * LANES:(j + 1) * LANES]
        return jnp.concatenate([scr_ref[j, t0:t0 + nt, :] for j in range(A_WIDTH // LANES)], axis=-1)

    for t0 in range(0, tm, MERGE_SUB):
        rows = slice(t0, t0 + MERGE_SUB)
        x = x_ref[0, rows, :]
        hb = _rms_norm(x, g_ref[...], NORM_EPS).astype(BF16)
        o4, l4 = (token_order(o4_ref, 4, s0_ref, t0, MERGE_SUB), token_order(l4_ref, 4, s1_ref, t0, MERGE_SUB))
        o16, l16 = (token_order(o16_ref, 16, s2_ref, t0, MERGE_SUB), token_order(l16_ref, 16, s3_ref, t0, MERGE_SUB))
        outs = [o1_ref[0, 0, rows, :].astype(F32), o4, o16]
        lses = [l1_ref[0, 0, rows, :], l4, l16]

        m = jnp.maximum(jnp.maximum(lses[0], lses[1]), lses[2])
        es = [jnp.exp2(v - m) for v in lses]
        num = es[0] * outs[0] + es[1] * outs[1] + es[2] * outs[2]
        attn_a = (num * (1.0 / (es[0] + es[1] + es[2]))).astype(BF16)

        ya = jnp.dot(attn_a, wa_ref[...], preferred_element_type=F32)
        yb = jnp.dot(ob_ref[0, rows, :], wb_ref[...], preferred_element_type=F32)
        ga = jnp.dot(hb, wg_ref[:, :d], preferred_element_type=F32)
        gb = jnp.dot(hb, wg_ref[:, d:], preferred_element_type=F32)
        merged = jax.nn.sigmoid(ga) * ya + jax.nn.sigmoid(gb) * yb
        out_ref[0, rows, :] = x + jnp.dot(merged.astype(BF16), wo_ref[...], preferred_element_type=F32)


def _merge(x, g, o1, l1, o4, l4, o16, l16, ob, wg, wa, wb, wo):
    b, s, d = x.shape
    tm = TOKEN_TILE
    tok = lambda w: pl.BlockSpec((1, tm, w), lambda i, j: (i, j, 0))
    phase = lambda dil, w: pl.BlockSpec((1, dil, tm // dil, w), lambda i, j: (i, 0, j, 0))
    const = lambda a: pl.BlockSpec(a.shape, lambda i, j: (0,) * a.ndim)
    return pl.pallas_call(
        _merge_kernel,
        grid=(b, s // tm),
        in_specs=[tok(d), const(g), phase(1, A_WIDTH), phase(1, A_WIDTH), phase(4, A_WIDTH), phase(4, A_WIDTH),
                  phase(16, A_WIDTH), phase(16, A_WIDTH), tok(B_WIDTH),
                  const(wg), const(wa), const(wb), const(wo)],
        out_specs=tok(d),
        out_shape=jax.ShapeDtypeStruct((b, s, d), F32),
        scratch_shapes=[pltpu.VMEM((A_WIDTH // LANES, tm, LANES), F32)] * 4,
        compiler_params=pltpu.CompilerParams(
            dimension_semantics=("arbitrary", "arbitrary"), vmem_limit_bytes=VMEM_LIMIT_BYTES),
        name="merge_proj",
    )(x, g, o1, l1, o4, l4, o16, l16, ob, wg, wa, wb, wo)


MLP_TF = 512
MLP_TM = 1024


def _mlp_kernel(x_ref, g_ref, w1_ref, w2_ref, gf_ref, out_ref):
    x = x_ref[...]
    hb = _rms_norm(x, g_ref[...], NORM_EPS).astype(BF16)
    acc = x
    for c in range(w1_ref.shape[1] // MLP_TF):
        cs = slice(c * MLP_TF, (c + 1) * MLP_TF)
        u = jnp.maximum(jnp.dot(hb, w1_ref[:, cs], preferred_element_type=F32), 0.0)
        acc = acc + jnp.dot((u * u).astype(BF16), w2_ref[cs, :], preferred_element_type=F32)
    out_ref[...] = _rms_norm(acc, gf_ref[...], NORM_EPS)


def _mlp(x2d, g, w1, w2, gf):
    t, d = x2d.shape
    tm = MLP_TM
    row = pl.BlockSpec((tm, d), lambda i: (i, 0))
    full = lambda a: pl.BlockSpec(a.shape, lambda i: (0, 0), pipeline_mode=pl.Buffered(1))
    return pl.pallas_call(
        _mlp_kernel,
        grid=(t // tm,),
        in_specs=[row, full(g), full(w1), full(w2), full(gf)],
        out_specs=row,
        out_shape=jax.ShapeDtypeStruct((t, d), F32),
        compiler_params=pltpu.CompilerParams(
            dimension_semantics=("arbitrary",), vmem_limit_bytes=VMEM_LIMIT_BYTES),
        name="mlp_norm",
    )(x2d, g, w1, w2, gf)


def _rope_tables(seq):
    pos = jnp.arange(seq, dtype=F32)
    inv_freq = ROPE_THETA ** (-jnp.arange(0, HEAD_DIM, 2, dtype=F32) / HEAD_DIM)
    ang = pos[:, None] * inv_freq[None, :]
    cos = jnp.cos(ang)
    sin = jnp.sin(ang)
    cos = jnp.concatenate([cos, cos, cos, cos], axis=-1)
    sin = jnp.concatenate([-sin, sin, -sin, sin], axis=-1)
    return cos, sin


def kernel(x, w_in, w_branch_a, w_branch_b, w_out, lambda_q1, lambda_k1, lambda_q2, lambda_k2,
           diff_subln_g, norm_mix_g, norm_mlp_g, w_ff1, w_ff2, norm_final_g):
    b, s, d = x.shape
    assert w_in.shape[0] == 1, "single-layer block: the final norm is fused into the MLP kernel"
    assert s % B_KB == 0 and s % MLP_TM == 0 and s % QKV_TS == 0
    cos, sin = _rope_tables(s)
    w_in_l = w_in[0].astype(BF16)
    g_mix = norm_mix_g[0].reshape(1, d)
    vb_lo = QKV_A_WIDTH + QK_B_WIDTH
    qkva, qkb, vbt, ph4, ph16 = _qkv_proj(
        x, g_mix, w_in_l[:, :vb_lo], w_in_l[:, vb_lo:QKV_WIDTH].T, cos, sin,
        jnp.asarray(_phase_perm(4), BF16), jnp.asarray(_phase_perm(16), BF16))
    o1, l1 = _band_attention(qkva.reshape(b, 1, s, QKV_A_WIDTH), DILATED_PATTERNS[0][0])
    o4, l4 = _band_attention(ph4, DILATED_PATTERNS[1][0])
    o16, l16 = _band_attention(ph16, DILATED_PATTERNS[2][0])
    lam_params = jnp.stack([lambda_q1[0], lambda_k1[0], lambda_q2[0], lambda_k2[0]]).astype(F32)
    ob = _diff_attention(qkb, vbt, lam_params, diff_subln_g[0].reshape(1, 2 * HEAD_DIM).astype(F32))
    x1 = _merge(x, g_mix, o1, l1, o4, l4, o16, l16, ob, w_in_l[:, QKV_WIDTH:],
                w_branch_a[0].astype(BF16), w_branch_b[0].astype(BF16), w_out[0].astype(BF16))
    out = _mlp(x1.reshape(b * s, d), norm_mlp_g[0].reshape(1, d), w_ff1[0].astype(BF16),
               w_ff2[0].astype(BF16), norm_final_g.reshape(1, d))
    return out.reshape(b, s, d)
```

```python
import functools
import math

import numpy as np
import jax
import jax.numpy as jnp
from jax import lax
from jax.experimental import pallas as pl
from jax.experimental.pallas import tpu as pltpu

HEAD_DIM = 64
HALF_DIM = HEAD_DIM // 2
LANES = 128
A_HEADS = 8
A_WIDTH = A_HEADS * HEAD_DIM
B_HEADS = 4
B_WIDTH = B_HEADS * 2 * HEAD_DIM
QKV_A_WIDTH = 3 * A_WIDTH
QK_B_WIDTH = 2 * B_WIDTH
QKV_WIDTH = QKV_A_WIDTH + QK_B_WIDTH + B_WIDTH
DILATED_PATTERNS = ((128, 1), (512, 4), (2048, 16))
ROPE_THETA = 10000.0
NORM_EPS = 1e-6
SUBLN_EPS = 1e-5
LAMBDA_INIT = 0.8 - 0.6 * math.exp(-0.3 * 0)
QK_SCALE_LOG2E = math.log2(math.e) / math.sqrt(HEAD_DIM)
MASK_VALUE = -math.inf
VMEM_LIMIT_BYTES = 56 * 1024 * 1024
PERM = 256
TOKEN_TILE = 1024
QKV_TS = 1024

BF16 = jnp.bfloat16
F32 = jnp.float32
NT_DIMS = (((1,), (1,)), ((), ()))


def _rms_norm(x, g, eps):
    return (x * lax.rsqrt(jnp.mean(x * x, axis=-1, keepdims=True) + eps)) * g


def _split_heads_rows(q):
    lane = lax.broadcasted_iota(jnp.int32, q.shape, 1)
    zero = jnp.zeros_like(q)
    return jnp.concatenate(
        [jnp.where(lane < HEAD_DIM, q, zero), jnp.where(lane >= HEAD_DIM, q, zero)], axis=0)


def _phase_perm(dil):
    n = PERM // dil
    p = np.zeros((PERM, PERM), np.float32)
    for r in range(dil):
        for i in range(n):
            p[r * n + i, i * dil + r] = 1.0
    return p


def _qkv_kernel(x_ref, g_ref, w_ref, wvt_ref, cos_ref, sin_ref, p4_ref, p16_ref,
                qkva_ref, qkb_ref, vbt_ref, ph4_ref, ph16_ref):
    x = x_ref[0]
    ts = x.shape[0]
    hb = _rms_norm(x, g_ref[...], NORM_EPS).astype(BF16)
    cos = cos_ref[...]
    sin = sin_ref[...]
    lane = lax.broadcasted_iota(jnp.int32, cos.shape, 1)
    first_half = (lane % HEAD_DIM) < HALF_DIM

    def rope(t, scale):
        parts = []
        for j in range(t.shape[1] // LANES):
            tj = t[:, j * LANES:(j + 1) * LANES]
            rot = jnp.where(first_half, pltpu.roll(tj, LANES - HALF_DIM, 1), pltpu.roll(tj, HALF_DIM, 1))
            tj = tj * cos + rot * sin
            parts.append((tj * scale if scale != 1.0 else tj).astype(BF16))
        return parts

    for c in range(5):
        t = jnp.dot(hb, w_ref[:, c * A_WIDTH:(c + 1) * A_WIDTH], preferred_element_type=F32)
        dst, base = (qkva_ref, c * A_WIDTH) if c < 3 else (qkb_ref, (c - 3) * A_WIDTH)
        if c == 2:
            dst[0, :, base:base + A_WIDTH] = t.astype(BF16)
        else:
            scale = QK_SCALE_LOG2E if c in (0, 3) else 1.0
            for j, tj in enumerate(rope(t, scale)):
                dst[0, :, base + j * LANES:base + (j + 1) * LANES] = tj
    vbt = lax.dot_general(wvt_ref[...], hb, NT_DIMS, preferred_element_type=F32).astype(BF16)
    for u in range(ts // B_TK):
        vbt_ref[0, u] = vbt[:, u * B_TK:(u + 1) * B_TK]

    for dil, p_ref, ph_ref in ((4, p4_ref, ph4_ref), (16, p16_ref, ph16_ref)):
        n = PERM // dil
        for u in range(ts // PERM):
            for c in range(3):
                cs = slice(c * A_WIDTH, (c + 1) * A_WIDTH)
                tb = qkva_ref[0, u * PERM:(u + 1) * PERM, cs]
                ph = jnp.dot(p_ref[...], tb, preferred_element_type=F32).astype(BF16)
                for r in range(dil):
                    ph_ref[0, r, u * n:(u + 1) * n, cs] = ph[r * n:(r + 1) * n]


def _qkv_proj(x, g, w_qkv, w_vbt, cos, sin, p4, p16):
    b, s, d = x.shape
    ts = QKV_TS
    const = lambda a: pl.BlockSpec(a.shape, lambda i, j: (0,) * a.ndim, pipeline_mode=pl.Buffered(1))
    return pl.pallas_call(
        _qkv_kernel,
        grid=(b, s // ts),
        in_specs=[
            pl.BlockSpec((1, ts, d), lambda i, j: (i, j, 0)),
            const(g), const(w_qkv), const(w_vbt),
            pl.BlockSpec((ts, LANES), lambda i, j: (j, 0)),
            pl.BlockSpec((ts, LANES), lambda i, j: (j, 0)),
            const(p4), const(p16),
        ],
        out_specs=[
            pl.BlockSpec((1, ts, QKV_A_WIDTH), lambda i, j: (i, j, 0)),
            pl.BlockSpec((1, ts, QK_B_WIDTH), lambda i, j: (i, j, 0)),
            pl.BlockSpec((1, ts // B_TK, B_WIDTH, B_TK), lambda i, j: (i, j, 0, 0)),
            pl.BlockSpec((1, 4, ts // 4, QKV_A_WIDTH), lambda i, j: (i, 0, j, 0)),
            pl.BlockSpec((1, 16, ts // 16, QKV_A_WIDTH), lambda i, j: (i, 0, j, 0)),
        ],
        out_shape=[
            jax.ShapeDtypeStruct((b, s, QKV_A_WIDTH), BF16),
            jax.ShapeDtypeStruct((b, s, QK_B_WIDTH), BF16),
            jax.ShapeDtypeStruct((b, s // B_TK, B_WIDTH, B_TK), BF16),
            jax.ShapeDtypeStruct((b, 4, s // 4, QKV_A_WIDTH), BF16),
            jax.ShapeDtypeStruct((b, 16, s // 16, QKV_A_WIDTH), BF16),
        ],
        compiler_params=pltpu.CompilerParams(
            dimension_semantics=("arbitrary", "arbitrary"), vmem_limit_bytes=VMEM_LIMIT_BYTES),
        name="qkv_proj",
    )(x, g, w_qkv, w_vbt, cos, sin, p4, p16)


A_TQ = 128
A_WIN = 256
BAND_ROWS = 2048


def _band_bias():
    row = np.arange(2 * A_TQ)[:, None] % A_TQ
    col = np.arange(A_WIN)[None, :]
    half = (A_WIN - A_TQ) // 2
    return np.stack([np.where(np.abs(col - row + d) <= half, 0.0, MASK_VALUE)
                     for d in (-half, 0, -2 * half)]).astype(np.float32)


def _band_kernel(q_ref, k_ref, v_ref, bias_ref, o_ref, lse_ref, *, half, ls, rows_per_step):
    step = pl.program_id(2)
    n_sub = rows_per_step // A_TQ
    n_phase = q_ref.shape[1]
    out_lane = lax.broadcasted_iota(jnp.int32, (A_TQ, LANES), 1)

    def sub_tile(u, carry):
        ph = u // n_sub
        r0 = pl.multiple_of((u % n_sub) * A_TQ, A_TQ)
        t0 = step * rows_per_step + r0
        start = pl.multiple_of(jnp.clip(t0 - half, 0, ls - A_WIN), half)
        which = jnp.where(t0 == 0, 1, jnp.where(t0 == ls - A_TQ, 2, 0))
        for j in range(A_WIDTH // LANES):
            cs = slice(j * LANES, (j + 1) * LANES)
            qq = _split_heads_rows(q_ref[0, ph, pl.ds(r0, A_TQ), cs])
            kw = k_ref[0, ph, pl.ds(start, A_WIN), cs]
            vw = v_ref[0, ph, pl.ds(start, A_WIN), cs]
            s = lax.dot_general(qq, kw, NT_DIMS, preferred_element_type=F32) + bias_ref[which]
            m = jnp.max(s, axis=-1, keepdims=True)
            p = jnp.exp2(s - m)
            l = jnp.sum(p, axis=-1, keepdims=True)
            pv = jnp.dot(p.astype(BF16), vw, preferred_element_type=F32) * (1.0 / l)
            o_ref[0, ph, pl.ds(r0, A_TQ), cs] = jnp.where(
                out_lane < HEAD_DIM, pv[:A_TQ], pv[A_TQ:]).astype(BF16)
            lse = m + jnp.log2(l)
            lse_ref[0, ph, pl.ds(r0, A_TQ), cs] = jnp.where(out_lane < HEAD_DIM, lse[:A_TQ], lse[A_TQ:])
        return carry

    lax.fori_loop(0, n_phase * n_sub, sub_tile, 0, unroll=16)


def _band_attention(ph, window):
    b, dil, ls, _ = ph.shape
    half = window // (2 * dil)
    assert half == A_TQ // 2 and A_WIN == A_TQ + 2 * half and ls % A_TQ == 0 and ls >= A_WIN
    rows_per_step = min(ls, BAND_ROWS)
    n_phase = BAND_ROWS // rows_per_step
    assert dil % n_phase == 0
    kern = functools.partial(_band_kernel, half=half, ls=ls, rows_per_step=rows_per_step)
    return pl.pallas_call(
        kern,
        grid=(b, dil // n_phase, ls // rows_per_step),
        in_specs=[
            pl.BlockSpec((1, n_phase, rows_per_step, A_WIDTH), lambda i, r, t: (i, r, t, 0)),
            pl.BlockSpec((1, n_phase, ls, A_WIDTH), lambda i, r, t: (i, r, 0, 1)),
            pl.BlockSpec((1, n_phase, ls, A_WIDTH), lambda i, r, t: (i, r, 0, 2)),
            pl.BlockSpec((3, 2 * A_TQ, A_WIN), lambda i, r, t: (0, 0, 0)),
        ],
        out_specs=[
            pl.BlockSpec((1, n_phase, rows_per_step, A_WIDTH), lambda i, r, t: (i, r, t, 0)),
            pl.BlockSpec((1, n_phase, rows_per_step, A_WIDTH), lambda i, r, t: (i, r, t, 0)),
        ],
        out_shape=[
            jax.ShapeDtypeStruct((b, dil, ls, A_WIDTH), BF16),
            jax.ShapeDtypeStruct((b, dil, ls, A_WIDTH), F32),
        ],
        compiler_params=pltpu.CompilerParams(
            dimension_semantics=("arbitrary", "arbitrary", "arbitrary"),
            vmem_limit_bytes=VMEM_LIMIT_BYTES),
        name=f"band_attn_d{dil}",
    )(ph, ph, ph, jnp.asarray(_band_bias()))


B_TQ = 256
B_KB = 1024
B_TK = 256


def _diff_kernel(q_ref, k_ref, vt_ref, lam_ref, g_ref, o_ref, sa_ref, sb_ref, *, seq):
    lam_p = lam_ref[...]
    lam = (jnp.exp(jnp.sum(lam_p[0:1] * lam_p[1:2], axis=-1, keepdims=True))
           - jnp.exp(jnp.sum(lam_p[2:3] * lam_p[3:4], axis=-1, keepdims=True)) + LAMBDA_INIT)
    g = g_ref[...]

    n_kb = seq // B_KB
    n_qt = seq // B_TQ

    def load_qq(t):
        return _split_heads_rows(q_ref[0, pl.ds(pl.multiple_of(t * B_TQ, B_TQ), B_TQ), :])

    def scores(t, s_ref):
        qq = load_qq(t)
        cmax = None
        for kb in range(n_kb):
            s = lax.dot_general(k_ref[0, kb * B_KB:(kb + 1) * B_KB, :], qq, NT_DIMS,
                                preferred_element_type=F32)
            s_ref[kb * B_KB:(kb + 1) * B_KB, :] = s
            bmax = jnp.max(s, axis=0, keepdims=True)
            cmax = bmax if cmax is None else jnp.maximum(cmax, bmax)
        return cmax

    def softmax_pv(s_ref, m):
        l = jnp.zeros((1, 2 * B_TQ), F32)
        acc = jnp.zeros((2 * HEAD_DIM, 2 * B_TQ), F32)
        for c in range(seq // B_TK):
            p = jnp.exp2(s_ref[c * B_TK:(c + 1) * B_TK, :] - m)
            l = l + jnp.sum(p, axis=0, keepdims=True)
            acc = acc + jnp.dot(vt_ref[0, c], p.astype(BF16), preferred_element_type=F32)
        return l, acc

    def finish(t, l_acc):
        l, acc = l_acc
        o = acc * (1.0 / l)
        o = o[:, :B_TQ] - lam * o[:, B_TQ:]
        o = o * lax.rsqrt(jnp.mean(o * o, axis=0, keepdims=True) + SUBLN_EPS)
        q0 = pl.multiple_of(t * B_TQ, B_TQ)
        o_ref[0, pl.ds(q0, B_TQ), :] = ((o.T * g) * (1.0 - LAMBDA_INIT)).astype(BF16)

    def tile_pair(j, carry):
        m_a, pending = carry
        m_b = scores(2 * j + 1, sb_ref)
        finish(jnp.maximum(2 * j - 1, 0), pending)
        pending = softmax_pv(sa_ref, m_a)
        m_a = scores(2 * j + 2, sa_ref)
        finish(2 * j, pending)
        return m_a, softmax_pv(sb_ref, m_b)

    assert n_qt % 2 == 0 and n_qt >= 4
    placeholder = (jnp.ones((1, 2 * B_TQ), F32), jnp.zeros((2 * HEAD_DIM, 2 * B_TQ), F32))
    m_a, pending = lax.fori_loop(0, n_qt // 2 - 1, tile_pair, (scores(0, sa_ref), placeholder))
    m_b = scores(n_qt - 1, sb_ref)
    finish(n_qt - 3, pending)
    finish(n_qt - 2, softmax_pv(sa_ref, m_a))
    finish(n_qt - 1, softmax_pv(sb_ref, m_b))


def _diff_attention(qkb, vbt, lam_params, subln_g):
    b, s, _ = qkb.shape
    return pl.pallas_call(
        functools.partial(_diff_kernel, seq=s),
        grid=(b, B_HEADS),
        in_specs=[
            pl.BlockSpec((1, s, LANES), lambda i, h: (i, 0, h)),
            pl.BlockSpec((1, s, LANES), lambda i, h: (i, 0, B_HEADS + h)),
            pl.BlockSpec((1, s // B_TK, LANES, B_TK), lambda i, h: (i, 0, h, 0)),
            pl.BlockSpec((4, HEAD_DIM), lambda i, h: (0, 0)),
            pl.BlockSpec((1, 2 * HEAD_DIM), lambda i, h: (0, 0)),
        ],
        out_specs=pl.BlockSpec((1, s, LANES), lambda i, h: (i, 0, h)),
        out_shape=jax.ShapeDtypeStruct((b, s, B_WIDTH), BF16),
        scratch_shapes=[pltpu.VMEM((s, 2 * B_TQ), F32), pltpu.VMEM((s, 2 * B_TQ), F32)],
        compiler_params=pltpu.CompilerParams(
            dimension_semantics=("arbitrary", "arbitrary"), vmem_limit_bytes=VMEM_LIMIT_BYTES),
        name="diff_attn",
    )(qkb, qkb, vbt, lam_params, subln_g)


MERGE_SUB = 256


def _merge_kernel(x_ref, g_ref, o1_ref, l1_ref, o4_ref, l4_ref, o16_ref, l16_ref, ob_ref,
                  wg_ref, wa_ref, wb_ref, wo_ref, out_ref, s0_ref, s1_ref, s2_ref, s3_ref):
    tm, d = x_ref.shape[1:]

    def token_order(src_ref, dil, scr_ref, t0, nt):
        n, i0 = nt // dil, t0 // dil
        for r in range(dil):
            blk = src_ref[0, r, i0:i0 + n, :].astype(F32)
            for j in range(A_WIDTH // LANES):
                scr_ref[j, pl.ds(t0 + r, n, stride=dil), :] = blk[:, j * LANES:(j + 1) * LANES]
        return jnp.concatenate([scr_ref[j, t0:t0 + nt, :] for j in range(A_WIDTH // LANES)], axis=-1)

    for t0 in range(0, tm, MERGE_SUB):
        rows = slice(t0, t0 + MERGE_SUB)
        x = x_ref[0, rows, :]
        hb = _rms_norm(x, g_ref[...], NORM_EPS).astype(BF16)
        o4, l4 = (token_order(o4_ref, 4, s0_ref, t0, MERGE_SUB), token_order(l4_ref, 4, s1_ref, t0, MERGE_SUB))
        o16, l16 = (token_order(o16_ref, 16, s2_ref, t0, MERGE_SUB), token_order(l16_ref, 16, s3_ref, t0, MERGE_SUB))
        outs = [o1_ref[0, 0, rows, :].astype(F32), o4, o16]
        lses = [l1_ref[0, 0, rows, :], l4, l16]

        m = jnp.maximum(jnp.maximum(lses[0], lses[1]), lses[2])
        es = [jnp.exp2(v - m) for v in lses]
        num = es[0] * outs[0] + es[1] * outs[1] + es[2] * outs[2]
        attn_a = (num * (1.0 / (es[0] + es[1] + es[2]))).astype(BF16)

        ya = jnp.dot(attn_a, wa_ref[...], preferred_element_type=F32)
        yb = jnp.dot(ob_ref[0, rows, :], wb_ref[...], preferred_element_type=F32)
        ga = jnp.dot(hb, wg_ref[:, :d], preferred_element_type=F32)
        gb = jnp.dot(hb, wg_ref[:, d:], preferred_element_type=F32)
        merged = jax.nn.sigmoid(ga) * ya + jax.nn.sigmoid(gb) * yb
        out_ref[0, rows, :] = x + jnp.dot(merged.astype(BF16), wo_ref[...], preferred_element_type=F32)


def _merge(x, g, o1, l1, o4, l4, o16, l16, ob, wg, wa, wb, wo):
    b, s, d = x.shape
    tm = TOKEN_TILE
    tok = lambda w: pl.BlockSpec((1, tm, w), lambda i, j: (i, j, 0))
    phase = lambda dil, w: pl.BlockSpec((1, dil, tm // dil, w), lambda i, j: (i, 0, j, 0))
    const = lambda a: pl.BlockSpec(a.shape, lambda i, j: (0,) * a.ndim, pipeline_mode=pl.Buffered(1))
    return pl.pallas_call(
        _merge_kernel,
        grid=(b, s // tm),
        in_specs=[tok(d), const(g), phase(1, A_WIDTH), phase(1, A_WIDTH), phase(4, A_WIDTH), phase(4, A_WIDTH),
                  phase(16, A_WIDTH), phase(16, A_WIDTH), tok(B_WIDTH),
                  const(wg), const(wa), const(wb), const(wo)],
        out_specs=tok(d),
        out_shape=jax.ShapeDtypeStruct((b, s, d), F32),
        scratch_shapes=[pltpu.VMEM((A_WIDTH // LANES, tm, LANES), F32)] * 4,
        compiler_params=pltpu.CompilerParams(
            dimension_semantics=("arbitrary", "arbitrary"), vmem_limit_bytes=VMEM_LIMIT_BYTES),
        name="merge_proj",
    )(x, g, o1, l1, o4, l4, o16, l16, ob, wg, wa, wb, wo)


MLP_TF = 512
MLP_TM = 1024


def _mlp_kernel(x_ref, g_ref, w1_ref, w2_ref, gf_ref, out_ref):
    x = x_ref[...]
    hb = _rms_norm(x, g_ref[...], NORM_EPS).astype(BF16)
    acc = x
    for c in range(w1_ref.shape[1] // MLP_TF):
        cs = slice(c * MLP_TF, (c + 1) * MLP_TF)
        u = jnp.maximum(jnp.dot(hb, w1_ref[:, cs], preferred_element_type=F32), 0.0)
        acc = acc + jnp.dot((u * u).astype(BF16), w2_ref[cs, :], preferred_element_type=F32)
    out_ref[...] = _rms_norm(acc, gf_ref[...], NORM_EPS)


def _mlp(x2d, g, w1, w2, gf):
    t, d = x2d.shape
    tm = MLP_TM
    row = pl.BlockSpec((tm, d), lambda i: (i, 0))
    full = lambda a: pl.BlockSpec(a.shape, lambda i: (0, 0), pipeline_mode=pl.Buffered(1))
    return pl.pallas_call(
        _mlp_kernel,
        grid=(t // tm,),
        in_specs=[row, full(g), full(w1), full(w2), full(gf)],
        out_specs=row,
        out_shape=jax.ShapeDtypeStruct((t, d), F32),
        compiler_params=pltpu.CompilerParams(
            dimension_semantics=("arbitrary",), vmem_limit_bytes=VMEM_LIMIT_BYTES),
        name="mlp_norm",
    )(x2d, g, w1, w2, gf)


def _rope_tables(seq):
    pos = jnp.arange(seq, dtype=F32)
    inv_freq = ROPE_THETA ** (-jnp.arange(0, HEAD_DIM, 2, dtype=F32) / HEAD_DIM)
    ang = pos[:, None] * inv_freq[None, :]
    cos = jnp.cos(ang)
    sin = jnp.sin(ang)
    cos = jnp.concatenate([cos, cos, cos, cos], axis=-1)
    sin = jnp.concatenate([-sin, sin, -sin, sin], axis=-1)
    return cos, sin


def kernel(x, w_in, w_branch_a, w_branch_b, w_out, lambda_q1, lambda_k1, lambda_q2, lambda_k2,
           diff_subln_g, norm_mix_g, norm_mlp_g, w_ff1, w_ff2, norm_final_g):
    b, s, d = x.shape
    assert w_in.shape[0] == 1, "single-layer block: the final norm is fused into the MLP kernel"
    assert s % B_KB == 0 and s % MLP_TM == 0 and s % QKV_TS == 0
    cos, sin = _rope_tables(s)
    w_in_l = w_in[0].astype(BF16)
    g_mix = norm_mix_g[0].reshape(1, d)
    vb_lo = QKV_A_WIDTH + QK_B_WIDTH
    qkva, qkb, vbt, ph4, ph16 = _qkv_proj(
        x, g_mix, w_in_l[:, :vb_lo], w_in_l[:, vb_lo:QKV_WIDTH].T, cos, sin,
        jnp.asarray(_phase_perm(4), BF16), jnp.asarray(_phase_perm(16), BF16))
    o1, l1 = _band_attention(qkva.reshape(b, 1, s, QKV_A_WIDTH), DILATED_PATTERNS[0][0])
    o4, l4 = _band_attention(ph4, DILATED_PATTERNS[1][0])
    o16, l16 = _band_attention(ph16, DILATED_PATTERNS[2][0])
    lam_params = jnp.stack([lambda_q1[0], lambda_k1[0], lambda_q2[0], lambda_k2[0]]).astype(F32)
    ob = _diff_attention(qkb, vbt, lam_params, diff_subln_g[0].reshape(1, 2 * HEAD_DIM).astype(F32))
    x1 = _merge(x, g_mix, o1, l1, o4, l4, o16, l16, ob, w_in_l[:, QKV_WIDTH:],
                w_branch_a[0].astype(BF16), w_branch_b[0].astype(BF16), w_out[0].astype(BF16))
    out = _mlp(x1.reshape(b * s, d), norm_mlp_g[0].reshape(1, d), w_ff1[0].astype(BF16),
               w_ff2[0].astype(BF16), norm_final_g.reshape(1, d))
    return out.reshape(b, s, d)
```

```python
import functools
import math

import numpy as np
import jax
import jax.numpy as jnp
from jax import lax
from jax.experimental import pallas as pl
from jax.experimental.pallas import tpu as pltpu

HEAD_DIM = 64
HALF_DIM = HEAD_DIM // 2
LANES = 128
A_HEADS = 8
A_WIDTH = A_HEADS * HEAD_DIM
B_HEADS = 4
B_WIDTH = B_HEADS * 2 * HEAD_DIM
QKV_A_WIDTH = 3 * A_WIDTH
QK_B_WIDTH = 2 * B_WIDTH
QKV_WIDTH = QKV_A_WIDTH + QK_B_WIDTH + B_WIDTH
DILATED_PATTERNS = ((128, 1), (512, 4), (2048, 16))
ROPE_THETA = 10000.0
NORM_EPS = 1e-6
SUBLN_EPS = 1e-5
LAMBDA_INIT = 0.8 - 0.6 * math.exp(-0.3 * 0)
QK_SCALE_LOG2E = math.log2(math.e) / math.sqrt(HEAD_DIM)
MASK_VALUE = -math.inf
VMEM_LIMIT_BYTES = 56 * 1024 * 1024
PERM = 256
TOKEN_TILE = 1024
QKV_TS = 1024

BF16 = jnp.bfloat16
F32 = jnp.float32
NT_DIMS = (((1,), (1,)), ((), ()))


def _rms_norm(x, g, eps):
    return (x * lax.rsqrt(jnp.mean(x * x, axis=-1, keepdims=True) + eps)) * g


def _split_heads_rows(q):
    lane = lax.broadcasted_iota(jnp.int32, q.shape, 1)
    zero = jnp.zeros_like(q)
    return jnp.concatenate(
        [jnp.where(lane < HEAD_DIM, q, zero), jnp.where(lane >= HEAD_DIM, q, zero)], axis=0)


def _phase_perm(dil):
    n = PERM // dil
    p = np.zeros((PERM, PERM), np.float32)
    for r in range(dil):
        for i in range(n):
            p[r * n + i, i * dil + r] = 1.0
    return p


def _qkv_kernel(x_ref, g_ref, w_ref, wvt_ref, cos_ref, sin_ref, p4_ref, p16_ref,
                qkva_ref, qkb_ref, vbt_ref, ph4_ref, ph16_ref):
    x = x_ref[0]
    ts = x.shape[0]
    hb = _rms_norm(x, g_ref[...], NORM_EPS).astype(BF16)
    cos = cos_ref[...]
    sin = sin_ref[...]
    lane = lax.broadcasted_iota(jnp.int32, cos.shape, 1)
    first_half = (lane % HEAD_DIM) < HALF_DIM

    def rope(t, scale):
        parts = []
        for j in range(t.shape[1] // LANES):
            tj = t[:, j * LANES:(j + 1) * LANES]
            rot = jnp.where(first_half, pltpu.roll(tj, LANES - HALF_DIM, 1), pltpu.roll(tj, HALF_DIM, 1))
            tj = tj * cos + rot * sin
            parts.append((tj * scale if scale != 1.0 else tj).astype(BF16))
        return parts

    for c in range(5):
        t = jnp.dot(hb, w_ref[:, c * A_WIDTH:(c + 1) * A_WIDTH], preferred_element_type=F32)
        dst, base = (qkva_ref, c * A_WIDTH) if c < 3 else (qkb_ref, (c - 3) * A_WIDTH)
        if c == 2:
            dst[0, :, base:base + A_WIDTH] = t.astype(BF16)
        else:
            scale = QK_SCALE_LOG2E if c in (0, 3) else 1.0
            for j, tj in enumerate(rope(t, scale)):
                dst[0, :, base + j * LANES:base + (j + 1) * LANES] = tj
    vbt = lax.dot_general(wvt_ref[...], hb, NT_DIMS, preferred_element_type=F32).astype(BF16)
    for u in range(ts // B_TK):
        vbt_ref[0, u] = vbt[:, u * B_TK:(u + 1) * B_TK]

    for dil, p_ref, ph_ref in ((4, p4_ref, ph4_ref), (16, p16_ref, ph16_ref)):
        n = PERM // dil
        for u in range(ts // PERM):
            for c in range(3):
                cs = slice(c * A_WIDTH, (c + 1) * A_WIDTH)
                tb = qkva_ref[0, u * PERM:(u + 1) * PERM, cs]
                ph = jnp.dot(p_ref[...], tb, preferred_element_type=F32).astype(BF16)
                for r in range(dil):
                    ph_ref[0, r, u * n:(u + 1) * n, cs] = ph[r * n:(r + 1) * n]


def _qkv_proj(x, g, w_qkv, w_vbt, cos, sin, p4, p16):
    b, s, d = x.shape
    ts = QKV_TS
    const = lambda a: pl.BlockSpec(a.shape, lambda i, j: (0,) * a.ndim, pipeline_mode=pl.Buffered(1))
    return pl.pallas_call(
        _qkv_kernel,
        grid=(b, s // ts),
        in_specs=[
            pl.BlockSpec((1, ts, d), lambda i, j: (i, j, 0)),
            const(g), const(w_qkv), const(w_vbt),
            pl.BlockSpec((ts, LANES), lambda i, j: (j, 0)),
            pl.BlockSpec((ts, LANES), lambda i, j: (j, 0)),
            const(p4), const(p16),
        ],
        out_specs=[
            pl.BlockSpec((1, ts, QKV_A_WIDTH), lambda i, j: (i, j, 0)),
            pl.BlockSpec((1, ts, QK_B_WIDTH), lambda i, j: (i, j, 0)),
            pl.BlockSpec((1, ts // B_TK, B_WIDTH, B_TK), lambda i, j: (i, j, 0, 0)),
            pl.BlockSpec((1, 4, ts // 4, QKV_A_WIDTH), lambda i, j: (i, 0, j, 0)),
            pl.BlockSpec((1, 16, ts // 16, QKV_A_WIDTH), lambda i, j: (i, 0, j, 0)),
        ],
        out_shape=[
            jax.ShapeDtypeStruct((b, s, QKV_A_WIDTH), BF16),
            jax.ShapeDtypeStruct((b, s, QK_B_WIDTH), BF16),
            jax.ShapeDtypeStruct((b, s // B_TK, B_WIDTH, B_TK), BF16),
            jax.ShapeDtypeStruct((b, 4, s // 4, QKV_A_WIDTH), BF16),
            jax.ShapeDtypeStruct((b, 16, s // 16, QKV_A_WIDTH), BF16),
        ],
        compiler_params=pltpu.CompilerParams(
            dimension_semantics=("arbitrary", "arbitrary"), vmem_limit_bytes=VMEM_LIMIT_BYTES),
        name="qkv_proj",
    )(x, g, w_qkv, w_vbt, cos, sin, p4, p16)


A_TQ = 128
A_WIN = 256
BAND_ROWS = 2048


def _band_bias():
    row = np.arange(2 * A_TQ)[:, None] % A_TQ
    col = np.arange(A_WIN)[None, :]
    half = (A_WIN - A_TQ) // 2
    return np.stack([np.where(np.abs(col - row + d) <= half, 0.0, MASK_VALUE)
                     for d in (-half, 0, -2 * half)]).astype(np.float32)


def _band_kernel(q_ref, k_ref, v_ref, bias_ref, o_ref, lse_ref, *, half, ls, rows_per_step):
    step = pl.program_id(2)
    n_sub = rows_per_step // A_TQ
    n_phase = q_ref.shape[1]
    out_lane = lax.broadcasted_iota(jnp.int32, (A_TQ, LANES), 1)

    def sub_tile(u, carry):
        ph = u // n_sub
        r0 = pl.multiple_of((u % n_sub) * A_TQ, A_TQ)
        t0 = step * rows_per_step + r0
        start = pl.multiple_of(jnp.clip(t0 - half, 0, ls - A_WIN), half)
        which = jnp.where(t0 == 0, 1, jnp.where(t0 == ls - A_TQ, 2, 0))
        for j in range(A_WIDTH // LANES):
            cs = slice(j * LANES, (j + 1) * LANES)
            qq = _split_heads_rows(q_ref[0, ph, pl.ds(r0, A_TQ), cs])
            kw = k_ref[0, ph, pl.ds(start, A_WIN), cs]
            vw = v_ref[0, ph, pl.ds(start, A_WIN), cs]
            s = lax.dot_general(qq, kw, NT_DIMS, preferred_element_type=F32) + bias_ref[which]
            m = jnp.max(s, axis=-1, keepdims=True)
            p = jnp.exp2(s - m)
            l = jnp.sum(p, axis=-1, keepdims=True)
            pv = jnp.dot(p.astype(BF16), vw, preferred_element_type=F32) * (1.0 / l)
            o_ref[0, ph, pl.ds(r0, A_TQ), cs] = jnp.where(
                out_lane < HEAD_DIM, pv[:A_TQ], pv[A_TQ:]).astype(BF16)
            lse = m + jnp.log2(l)
            lse_ref[0, ph, pl.ds(r0, A_TQ), cs] = jnp.where(out_lane < HEAD_DIM, lse[:A_TQ], lse[A_TQ:])
        return carry

    lax.fori_loop(0, n_phase * n_sub, sub_tile, 0, unroll=16)


def _band_attention(ph, window):
    b, dil, ls, _ = ph.shape
    half = window // (2 * dil)
    assert half == A_TQ // 2 and A_WIN == A_TQ + 2 * half and ls % A_TQ == 0 and ls >= A_WIN
    rows_per_step = min(ls, BAND_ROWS)
    n_phase = BAND_ROWS // rows_per_step
    assert dil % n_phase == 0
    kern = functools.partial(_band_kernel, half=half, ls=ls, rows_per_step=rows_per_step)
    return pl.pallas_call(
        kern,
        grid=(b, dil // n_phase, ls // rows_per_step),
        in_specs=[
            pl.BlockSpec((1, n_phase, rows_per_step, A_WIDTH), lambda i, r, t: (i, r, t, 0)),
            pl.BlockSpec((1, n_phase, ls, A_WIDTH), lambda i, r, t: (i, r, 0, 1)),
            pl.BlockSpec((1, n_phase, ls, A_WIDTH), lambda i, r, t: (i, r, 0, 2)),
            pl.BlockSpec((3, 2 * A_TQ, A_WIN), lambda i, r, t: (0, 0, 0)),
        ],
        out_specs=[
            pl.BlockSpec((1, n_phase, rows_per_step, A_WIDTH), lambda i, r, t: (i, r, t, 0)),
            pl.BlockSpec((1, n_phase, rows_per_step, A_WIDTH), lambda i, r, t: (i, r, t, 0)),
        ],
        out_shape=[
            jax.ShapeDtypeStruct((b, dil, ls, A_WIDTH), BF16),
            jax.ShapeDtypeStruct((b, dil, ls, A_WIDTH), F32),
        ],
        compiler_params=pltpu.CompilerParams(
            dimension_semantics=("arbitrary", "arbitrary", "arbitrary"),
            vmem_limit_bytes=VMEM_LIMIT_BYTES),
        name=f"band_attn_d{dil}",
    )(ph, ph, ph, jnp.asarray(_band_bias()))


B_TQ = 256
B_KB = 1024
B_TK = 256


def _diff_kernel(q_ref, k_ref, vt_ref, lam_ref, g_ref, o_ref, sa_ref, sb_ref, *, seq):
    lam_p = lam_ref[...]
    lam = (jnp.exp(jnp.sum(lam_p[0:1] * lam_p[1:2], axis=-1, keepdims=True))
           - jnp.exp(jnp.sum(lam_p[2:3] * lam_p[3:4], axis=-1, keepdims=True)) + LAMBDA_INIT)
    g = g_ref[...]

    n_kb = seq // B_KB
    n_qt = seq // B_TQ

    def load_qq(t):
        return _split_heads_rows(q_ref[0, pl.ds(pl.multiple_of(t * B_TQ, B_TQ), B_TQ), :])

    def scores(t, s_ref):
        qq = load_qq(t)
        cmax = None
        for kb in range(n_kb):
            s = lax.dot_general(k_ref[0, kb * B_KB:(kb + 1) * B_KB, :], qq, NT_DIMS,
                                preferred_element_type=F32)
            s_ref[kb * B_KB:(kb + 1) * B_KB, :] = s
            bmax = jnp.max(s, axis=0, keepdims=True)
            cmax = bmax if cmax is None else jnp.maximum(cmax, bmax)
        return cmax

    def softmax_pv(s_ref, m):
        ls, accs = [], []
        for h in range(2):
            cols = slice(h * B_TQ, (h + 1) * B_TQ)
            mh = m[:, cols]
            l = jnp.zeros((1, B_TQ), F32)
            acc = jnp.zeros((2 * HEAD_DIM, B_TQ), F32)
            for c in range(seq // B_TK):
                p = jnp.exp2(s_ref[c * B_TK:(c + 1) * B_TK, cols] - mh)
                l = l + jnp.sum(p, axis=0, keepdims=True)
                acc = acc + jnp.dot(vt_ref[0, c], p.astype(BF16), preferred_element_type=F32)
            ls.append(l)
            accs.append(acc)
        return jnp.concatenate(ls, axis=1), jnp.concatenate(accs, axis=1)

    def finish(t, l_acc):
        l, acc = l_acc
        o = acc * (1.0 / l)
        o = o[:, :B_TQ] - lam * o[:, B_TQ:]
        o = o * lax.rsqrt(jnp.mean(o * o, axis=0, keepdims=True) + SUBLN_EPS)
        q0 = pl.multiple_of(t * B_TQ, B_TQ)
        o_ref[0, pl.ds(q0, B_TQ), :] = ((o.T * g) * (1.0 - LAMBDA_INIT)).astype(BF16)

    def tile_pair(j, carry):
        m_a, pending = carry
        m_b = scores(2 * j + 1, sb_ref)
        finish(jnp.maximum(2 * j - 1, 0), pending)
        pending = softmax_pv(sa_ref, m_a)
        m_a = scores(2 * j + 2, sa_ref)
        finish(2 * j, pending)
        return m_a, softmax_pv(sb_ref, m_b)

    assert n_qt % 2 == 0 and n_qt >= 4
    placeholder = (jnp.ones((1, 2 * B_TQ), F32), jnp.zeros((2 * HEAD_DIM, 2 * B_TQ), F32))
    m_a, pending = lax.fori_loop(0, n_qt // 2 - 1, tile_pair, (scores(0, sa_ref), placeholder))
    m_b = scores(n_qt - 1, sb_ref)
    finish(n_qt - 3, pending)
    finish(n_qt - 2, softmax_pv(sa_ref, m_a))
    finish(n_qt - 1, softmax_pv(sb_ref, m_b))


def _diff_attention(qkb, vbt, lam_params, subln_g):
    b, s, _ = qkb.shape
    return pl.pallas_call(
        functools.partial(_diff_kernel, seq=s),
        grid=(b, B_HEADS),
        in_specs=[
            pl.BlockSpec((1, s, LANES), lambda i, h: (i, 0, h)),
            pl.BlockSpec((1, s, LANES), lambda i, h: (i, 0, B_HEADS + h)),
            pl.BlockSpec((1, s // B_TK, LANES, B_TK), lambda i, h: (i, 0, h, 0)),
            pl.BlockSpec((4, HEAD_DIM), lambda i, h: (0, 0)),
            pl.BlockSpec((1, 2 * HEAD_DIM), lambda i, h: (0, 0)),
        ],
        out_specs=pl.BlockSpec((1, s, LANES), lambda i, h: (i, 0, h)),
        out_shape=jax.ShapeDtypeStruct((b, s, B_WIDTH), BF16),
        scratch_shapes=[pltpu.VMEM((s, 2 * B_TQ), F32), pltpu.VMEM((s, 2 * B_TQ), F32)],
        compiler_params=pltpu.CompilerParams(
            dimension_semantics=("arbitrary", "arbitrary"), vmem_limit_bytes=VMEM_LIMIT_BYTES),
        name="diff_attn",
    )(qkb, qkb, vbt, lam_params, subln_g)


MERGE_SUB = 256


def _merge_kernel(x_ref, g_ref, o1_ref, l1_ref, o4_ref, l4_ref, o16_ref, l16_ref, ob_ref,
                  wg_ref, wa_ref, wb_ref, wo_ref, out_ref, s0_ref, s1_ref, s2_ref, s3_ref):
    tm, d = x_ref.shape[1:]

    def token_order(src_ref, dil, scr_ref, t0, nt):
        n, i0 = nt // dil, t0 // dil
        for r in range(dil):
            blk = src_ref[0, r, i0:i0 + n, :].astype(F32)
            for j in range(A_WIDTH // LANES):
                scr_ref[j, pl.ds(t0 + r, n, stride=dil), :] = blk[:, j * LANES:(j + 1) * LANES]
        return jnp.concatenate([scr_ref[j, t0:t0 + nt, :] for j in range(A_WIDTH // LANES)], axis=-1)

    for t0 in range(0, tm, MERGE_SUB):
        rows = slice(t0, t0 + MERGE_SUB)
        x = x_ref[0, rows, :]
        hb = _rms_norm(x, g_ref[...], NORM_EPS).astype(BF16)
        o4, l4 = (token_order(o4_ref, 4, s0_ref, t0, MERGE_SUB), token_order(l4_ref, 4, s1_ref, t0, MERGE_SUB))
        o16, l16 = (token_order(o16_ref, 16, s2_ref, t0, MERGE_SUB), token_order(l16_ref, 16, s3_ref, t0, MERGE_SUB))
        outs = [o1_ref[0, 0, rows, :].astype(F32), o4, o16]
        lses = [l1_ref[0, 0, rows, :], l4, l16]

        m = jnp.maximum(jnp.maximum(lses[0], lses[1]), lses[2])
        es = [jnp.exp2(v - m) for v in lses]
        num = es[0] * outs[0] + es[1] * outs[1] + es[2] * outs[2]
        attn_a = (num * (1.0 / (es[0] + es[1] + es[2]))).astype(BF16)

        ya = jnp.dot(attn_a, wa_ref[...], preferred_element_type=F32)
        yb = jnp.dot(ob_ref[0, rows, :], wb_ref[...], preferred_element_type=F32)
        ga = jnp.dot(hb, wg_ref[:, :d], preferred_element_type=F32)
        gb = jnp.dot(hb, wg_ref[:, d:], preferred_element_type=F32)
        merged = jax.nn.sigmoid(ga) * ya + jax.nn.sigmoid(gb) * yb
        out_ref[0, rows, :] = x + jnp.dot(merged.astype(BF16), wo_ref[...], preferred_element_type=F32)


def _merge(x, g, o1, l1, o4, l4, o16, l16, ob, wg, wa, wb, wo):
    b, s, d = x.shape
    tm = TOKEN_TILE
    tok = lambda w: pl.BlockSpec((1, tm, w), lambda i, j: (i, j, 0))
    phase = lambda dil, w: pl.BlockSpec((1, dil, tm // dil, w), lambda i, j: (i, 0, j, 0))
    const = lambda a: pl.BlockSpec(a.shape, lambda i, j: (0,) * a.ndim, pipeline_mode=pl.Buffered(1))
    return pl.pallas_call(
        _merge_kernel,
        grid=(b, s // tm),
        in_specs=[tok(d), const(g), phase(1, A_WIDTH), phase(1, A_WIDTH), phase(4, A_WIDTH), phase(4, A_WIDTH),
                  phase(16, A_WIDTH), phase(16, A_WIDTH), tok(B_WIDTH),
                  const(wg), const(wa), const(wb), const(wo)],
        out_specs=tok(d),
        out_shape=jax.ShapeDtypeStruct((b, s, d), F32),
        scratch_shapes=[pltpu.VMEM((A_WIDTH // LANES, tm, LANES), F32)] * 4,
        compiler_params=pltpu.CompilerParams(
            dimension_semantics=("arbitrary", "arbitrary"), vmem_limit_bytes=VMEM_LIMIT_BYTES),
        name="merge_proj",
    )(x, g, o1, l1, o4, l4, o16, l16, ob, wg, wa, wb, wo)


MLP_TF = 512
MLP_TM = 1024


def _mlp_kernel(x_ref, g_ref, w1_ref, w2_ref, gf_ref, out_ref):
    x = x_ref[...]
    hb = _rms_norm(x, g_ref[...], NORM_EPS).astype(BF16)
    acc = x
    for c in range(w1_ref.shape[1] // MLP_TF):
        cs = slice(c * MLP_TF, (c + 1) * MLP_TF)
        u = jnp.maximum(jnp.dot(hb, w1_ref[:, cs], preferred_element_type=F32), 0.0)
        acc = acc + jnp.dot((u * u).astype(BF16), w2_ref[cs, :], preferred_element_type=F32)
    out_ref[...] = _rms_norm(acc, gf_ref[...], NORM_EPS)


def _mlp(x2d, g, w1, w2, gf):
    t, d = x2d.shape
    tm = MLP_TM
    row = pl.BlockSpec((tm, d), lambda i: (i, 0))
    full = lambda a: pl.BlockSpec(a.shape, lambda i: (0, 0), pipeline_mode=pl.Buffered(1))
    return pl.pallas_call(
        _mlp_kernel,
        grid=(t // tm,),
        in_specs=[row, full(g), full(w1), full(w2), full(gf)],
        out_specs=row,
        out_shape=jax.ShapeDtypeStruct((t, d), F32),
        compiler_params=pltpu.CompilerParams(
            dimension_semantics=("arbitrary",), vmem_limit_bytes=VMEM_LIMIT_BYTES),
        name="mlp_norm",
    )(x2d, g, w1, w2, gf)


def _rope_tables(seq):
    pos = jnp.arange(seq, dtype=F32)
    inv_freq = ROPE_THETA ** (-jnp.arange(0, HEAD_DIM, 2, dtype=F32) / HEAD_DIM)
    ang = pos[:, None] * inv_freq[None, :]
    cos = jnp.cos(ang)
    sin = jnp.sin(ang)
    cos = jnp.concatenate([cos, cos, cos, cos], axis=-1)
    sin = jnp.concatenate([-sin, sin, -sin, sin], axis=-1)
    return cos, sin


def kernel(x, w_in, w_branch_a, w_branch_b, w_out, lambda_q1, lambda_k1, lambda_q2, lambda_k2,
           diff_subln_g, norm_mix_g, norm_mlp_g, w_ff1, w_ff2, norm_final_g):
    b, s, d = x.shape
    assert w_in.shape[0] == 1, "single-layer block: the final norm is fused into the MLP kernel"
    assert s % B_KB == 0 and s % MLP_TM == 0 and s % QKV_TS == 0
    cos, sin = _rope_tables(s)
    w_in_l = w_in[0].astype(BF16)
    g_mix = norm_mix_g[0].reshape(1, d)
    vb_lo = QKV_A_WIDTH + QK_B_WIDTH
    qkva, qkb, vbt, ph4, ph16 = _qkv_proj(
        x, g_mix, w_in_l[:, :vb_lo], w_in_l[:, vb_lo:QKV_WIDTH].T, cos, sin,
        jnp.asarray(_phase_perm(4), BF16), jnp.asarray(_phase_perm(16), BF16))
    o1, l1 = _band_attention(qkva.reshape(b, 1, s, QKV_A_WIDTH), DILATED_PATTERNS[0][0])
    o4, l4 = _band_attention(ph4, DILATED_PATTERNS[1][0])
    o16, l16 = _band_attention(ph16, DILATED_PATTERNS[2][0])
    lam_params = jnp.stack([lambda_q1[0], lambda_k1[0], lambda_q2[0], lambda_k2[0]]).astype(F32)
    ob = _diff_attention(qkb, vbt, lam_params, diff_subln_g[0].reshape(1, 2 * HEAD_DIM).astype(F32))
    x1 = _merge(x, g_mix, o1, l1, o4, l4, o16, l16, ob, w_in_l[:, QKV_WIDTH:],
                w_branch_a[0].astype(BF16), w_branch_b[0].astype(BF16), w_out[0].astype(BF16))
    out = _mlp(x1.reshape(b * s, d), norm_mlp_g[0].reshape(1, d), w_ff1[0].astype(BF16),
               w_ff2[0].astype(BF16), norm_final_g.reshape(1, d))
    return out.reshape(b, s, d)
```
